```python
import jax, jax.numpy as jnp
from jax import lax
import numpy as np

D_MODEL = 1024
BATCH = 4
SEQ = 8192
DEPTH = 1
DEC_BATCH = 8
DEC_SEQ = 64
PAST_LEN = 2048

CHUNK = 64
HEAD_DIM = 64
RWKV_HEADS = 8
FOX_HEADS = 8
RWKV_WIDTH = RWKV_HEADS * HEAD_DIM
FOX_WIDTH = FOX_HEADS * HEAD_DIM
MIX_WIDTH = RWKV_WIDTH + FOX_WIDTH
DECAY_LORA = 64
AAA_LORA = 64
GATE_LORA = 128
RWKV_COLS = 3 * RWKV_WIDTH + DECAY_LORA + AAA_LORA + GATE_LORA
FOX_COLS = 3 * FOX_WIDTH + FOX_HEADS
IN_COLS = RWKV_COLS + FOX_COLS
RWKV_SPLITS = (RWKV_WIDTH, 2 * RWKV_WIDTH, 3 * RWKV_WIDTH,
               3 * RWKV_WIDTH + DECAY_LORA, 3 * RWKV_WIDTH + DECAY_LORA + AAA_LORA)
FOX_SPLITS = (FOX_WIDTH, 2 * FOX_WIDTH, 3 * FOX_WIDTH)
D_FF = -(-8 * D_MODEL // (3 * 256)) * 256
Q_BLOCK = 128
NORM_EPS = 1e-6
GN_EPS = 64e-5
FORGET_BIAS_INIT = 2.0

kernel_name = 'hymba_rwkv7_fox_adaln_stream_step'


def _rms_norm(x, g):
    x32 = x.astype(jnp.float32)
    y = x32 * lax.rsqrt(jnp.mean(x32 * x32, axis=-1, keepdims=True) + NORM_EPS)
    return (y * g.astype(jnp.float32)).astype(x.dtype)


def _heads(z, n_heads):
    return z.reshape(z.shape[:-1] + (n_heads, HEAD_DIM))


def _rwkv_scan(r, decay, k, v, kk, a, s0):
    def step(s, inp):
        r_t, w_t, k_t, v_t, kk_t, a_t = inp
        sa = jnp.einsum('bhvk,bhk->bhv', s, -kk_t)
        s = (s * w_t[:, :, None, :]
             + sa[..., None] * (kk_t * a_t)[:, :, None, :]
             + v_t[..., None] * k_t[:, :, None, :])
        return s, jnp.einsum('bhvk,bhk->bhv', s, r_t)
    xs = tuple(jnp.moveaxis(z, 1, 0) for z in (r, decay, k, v, kk, a))
    s_final, ys = lax.scan(step, s0, xs)
    return jnp.moveaxis(ys, 0, 1), s_final


def _fox_attend(q, k, v, lc_q, lc_k, pos_q, pos_k):
    s = jnp.einsum('bqhd,bshd->bhqs', q, k, preferred_element_type=jnp.float32) * (HEAD_DIM ** -0.5)
    decay_bias = jnp.transpose(lc_q, (0, 2, 1))[..., :, None] - jnp.transpose(lc_k, (0, 2, 1))[..., None, :]
    causal = pos_k[None, :] <= pos_q[:, None]
    p = jax.nn.softmax(jnp.where(causal, s + decay_bias, -jnp.inf), axis=-1)
    return jnp.einsum('bhqs,bshd->bqhd', p.astype(v.dtype), v)


def _fox_mixer(q, k_all, v_all, lc_all, past):
    b, t = q.shape[:2]
    lc_q = lc_all[:, past:]
    pos_k = jnp.arange(k_all.shape[1])
    pos_q = past + jnp.arange(t)
    if t <= Q_BLOCK:
        return _fox_attend(q, k_all, v_all, lc_q, lc_all, pos_q, pos_k)
    nb = t // Q_BLOCK
    qb = jnp.moveaxis(q.reshape(b, nb, Q_BLOCK, FOX_HEADS, HEAD_DIM), 1, 0)
    lb = jnp.moveaxis(lc_q.reshape(b, nb, Q_BLOCK, FOX_HEADS), 1, 0)
    pb = pos_q.reshape(nb, Q_BLOCK)
    out = lax.map(lambda blk: _fox_attend(blk[0], k_all, v_all, blk[1], lc_all, blk[2], pos_k), (qb, lb, pb))
    return jnp.moveaxis(out, 0, 1).reshape(q.shape)


def _layer(x, c, shift_prev, s_prev, k_past, v_past, logf_past,
           norm1_g, w_ada, b_ada, w_in, shift_mu, w0, w_decay_up, a0, w_aaa_up, w_gate_up,
           k_k, k_a, r_k, gn_g, gn_b, fox_q_g, fox_k_g, fox_f_b, w_out,
           norm2_g, w_ffn_gate, w_ffn_up, w_ffn_down):
    f32 = jnp.float32
    b, t, _ = x.shape
    past = k_past.shape[1]
    mod = jax.nn.silu(c) @ w_ada + b_ada
    sh1, sc1, gt1, sh2, sc2, gt2 = jnp.split(mod, 6, axis=-1)
    h = _rms_norm(x, norm1_g) * (1 + sc1[:, None]) + sh1[:, None]
    proj = h @ w_in
    p_rwkv = proj[..., :RWKV_COLS]
    p_fox = proj[..., RWKV_COLS:]

    prev = jnp.concatenate([shift_prev.astype(p_rwkv.dtype), p_rwkv[:, :-1]], axis=1)
    z = (p_rwkv + (prev - p_rwkv) * shift_mu).astype(f32)
    new_shift = p_rwkv[:, -1:]
    r, k, v, dw, da, dg = jnp.split(z, list(RWKV_SPLITS), axis=-1)
    w_log = -jax.nn.softplus(-(w0 + jnp.tanh(dw) @ w_decay_up)) - 0.5
    decay = jnp.exp(-jnp.exp(w_log))
    a = jax.nn.sigmoid(a0 + da @ w_aaa_up)
    g = jax.nn.sigmoid(dg) @ w_gate_up
    kk = _heads(k * k_k, RWKV_HEADS)
    kk = kk / jnp.maximum(jnp.linalg.norm(kk, axis=-1, keepdims=True), 1e-12)
    k = k * (1 + (a - 1) * k_a)
    rh, kh, vh = _heads(r, RWKV_HEADS), _heads(k, RWKV_HEADS), _heads(v, RWKV_HEADS)
    y, s_new = _rwkv_scan(rh, _heads(decay, RWKV_HEADS), kh, vh, kk, _heads(a, RWKV_HEADS),
                          s_prev.astype(f32))
    mu = jnp.mean(y, axis=-1, keepdims=True)
    var = jnp.mean(jnp.square(y - mu), axis=-1, keepdims=True)
    y = ((y - mu) * lax.rsqrt(var + GN_EPS)).reshape(b, t, RWKV_WIDTH) * gn_g + gn_b
    bonus = (jnp.sum(rh * kh * r_k, axis=-1, keepdims=True) * vh).reshape(b, t, RWKV_WIDTH)
    y_rwkv = ((y + bonus) * g).astype(x.dtype)

    q, kf, vf, fl = jnp.split(p_fox, list(FOX_SPLITS), axis=-1)
    q = _rms_norm(_heads(q, FOX_HEADS), fox_q_g)
    kf = _rms_norm(_heads(kf, FOX_HEADS), fox_k_g)
    vf = _heads(vf, FOX_HEADS)
    logf = jax.nn.log_sigmoid((fl + fox_f_b).astype(f32))
    k_all = jnp.concatenate([k_past.astype(kf.dtype), kf], axis=1)
    v_all = jnp.concatenate([v_past.astype(vf.dtype), vf], axis=1)
    lc_all = jnp.cumsum(jnp.concatenate([logf_past.astype(f32), logf], axis=1), axis=1)
    y_fox = _fox_mixer(q, k_all, v_all, lc_all, past).reshape(b, t, FOX_WIDTH).astype(x.dtype)

    mix = jnp.concatenate([y_rwkv, y_fox], axis=-1) @ w_out
    x = x + gt1[:, None] * mix
    h2 = _rms_norm(x, norm2_g) * (1 + sc2[:, None]) + sh2[:, None]
    ffn = (jax.nn.silu(h2 @ w_ffn_gate) * (h2 @ w_ffn_up)) @ w_ffn_down
    x = x + gt2[:, None] * ffn
    return (x, s_new.astype(s_prev.dtype), new_shift.astype(shift_prev.dtype), kf, vf,
            logf.astype(logf_past.dtype))


def setup_inputs(seed: int = 0) -> dict:
    key = jax.random.key(seed)
    ks = jax.random.split(key, 32)
    f32 = jnp.float32

    def nrm(i, shape, s=1.0):
        return s * jax.random.normal(ks[i], shape, f32)

    L = DEPTH
    return {
        'x_prompt': nrm(0, (BATCH, SEQ, D_MODEL)),
        'x_sample': nrm(1, (DEC_BATCH, DEC_SEQ, D_MODEL)),
        'cache_fox_k': nrm(2, (L, DEC_BATCH, PAST_LEN, FOX_HEADS, HEAD_DIM)),
        'cache_fox_v': nrm(3, (L, DEC_BATCH, PAST_LEN, FOX_HEADS, HEAD_DIM)),
        'cache_fox_logf': jax.nn.log_sigmoid(FORGET_BIAS_INIT + nrm(4, (L, DEC_BATCH, PAST_LEN, FOX_HEADS))),
        'state_rwkv': nrm(5, (L, DEC_BATCH, RWKV_HEADS, HEAD_DIM, HEAD_DIM), 0.5),
        'state_rwkv_shift': nrm(6, (L, DEC_BATCH, 1, RWKV_COLS)),
        'c_prompt': nrm(7, (BATCH, D_MODEL)),
        'c_sample': nrm(8, (DEC_BATCH, D_MODEL)),
        'norm1_g': 1.0 + nrm(9, (L, D_MODEL), 0.1),
        'w_ada': nrm(10, (L, D_MODEL, 6 * D_MODEL), 0.5 * D_MODEL ** -0.5),
        'b_ada': nrm(11, (L, 6 * D_MODEL), 0.01),
        'w_in': nrm(12, (L, D_MODEL, IN_COLS), D_MODEL ** -0.5),
        'shift_mu': jax.random.uniform(ks[13], (L, RWKV_COLS), f32),
        'w0': nrm(14, (L, RWKV_WIDTH), 0.5),
        'w_decay_up': nrm(15, (L, DECAY_LORA, RWKV_WIDTH), 0.5 * DECAY_LORA ** -0.5),
        'a0': nrm(16, (L, RWKV_WIDTH), 0.5),
        'w_aaa_up': nrm(17, (L, AAA_LORA, RWKV_WIDTH), AAA_LORA ** -0.5),
        'w_gate_up': nrm(18, (L, GATE_LORA, RWKV_WIDTH), GATE_LORA ** -0.5),
        'k_k': 1.0 + nrm(19, (L, RWKV_WIDTH), 0.1),
        'k_a': 1.0 + nrm(20, (L, RWKV_WIDTH), 0.1),
        'r_k': nrm(21, (L, RWKV_HEADS, HEAD_DIM), 0.1),
        'gn_g': 1.0 + nrm(22, (L, RWKV_WIDTH), 0.1),
        'gn_b': nrm(23, (L, RWKV_WIDTH), 0.01),
        'fox_q_g': 1.0 + nrm(24, (L, HEAD_DIM), 0.1),
        'fox_k_g': 1.0 + nrm(25, (L, HEAD_DIM), 0.1),
        'fox_f_b': FORGET_BIAS_INIT + nrm(26, (L, FOX_HEADS), 0.5),
        'w_out': nrm(27, (L, MIX_WIDTH, D_MODEL), MIX_WIDTH ** -0.5),
        'norm2_g': 1.0 + nrm(28, (L, D_MODEL), 0.1),
        'w_ffn_gate': nrm(29, (L, D_MODEL, D_FF), D_MODEL ** -0.5),
        'w_ffn_up': nrm(30, (L, D_MODEL, D_FF), D_MODEL ** -0.5),
        'w_ffn_down': nrm(31, (L, D_FF, D_MODEL), D_FF ** -0.5),
    }


def reference(x_prompt, x_sample, cache_fox_k, cache_fox_v, cache_fox_logf, state_rwkv, state_rwkv_shift,
              c_prompt, c_sample, norm1_g, w_ada, b_ada, w_in, shift_mu, w0, w_decay_up, a0, w_aaa_up,
              w_gate_up, k_k, k_a, r_k, gn_g, gn_b, fox_q_g, fox_k_g, fox_f_b, w_out, norm2_g,
              w_ffn_gate, w_ffn_up, w_ffn_down):
    assert x_sample.shape[1] <= CHUNK
    bp = x_prompt.shape[0]
    zero_shift = jnp.zeros((bp, 1, RWKV_COLS), state_rwkv_shift.dtype)
    zero_state = jnp.zeros((bp, RWKV_HEADS, HEAD_DIM, HEAD_DIM), state_rwkv.dtype)
    zero_kv = jnp.zeros((bp, 0, FOX_HEADS, HEAD_DIM), cache_fox_k.dtype)
    zero_logf = jnp.zeros((bp, 0, FOX_HEADS), cache_fox_logf.dtype)
    hp, hs = x_prompt, x_sample
    rs_p, sh_p, k_p, v_p, lf_p = [], [], [], [], []
    rs_s, sh_s, k_s, v_s, lf_s = [], [], [], [], []
    for l in range(DEPTH):
        lw = (norm1_g[l], w_ada[l], b_ada[l], w_in[l], shift_mu[l], w0[l], w_decay_up[l], a0[l],
              w_aaa_up[l], w_gate_up[l], k_k[l], k_a[l], r_k[l], gn_g[l], gn_b[l], fox_q_g[l],
              fox_k_g[l], fox_f_b[l], w_out[l], norm2_g[l], w_ffn_gate[l], w_ffn_up[l], w_ffn_down[l])
        hp, s1, s2, s3, s4, s5 = _layer(hp, c_prompt, zero_shift, zero_state, zero_kv, zero_kv, zero_logf, *lw)
        rs_p.append(s1); sh_p.append(s2); k_p.append(s3); v_p.append(s4); lf_p.append(s5)
        hs, u1, u2, u3, u4, u5 = _layer(hs, c_sample, state_rwkv_shift[l], state_rwkv[l], cache_fox_k[l],
                                        cache_fox_v[l], cache_fox_logf[l], *lw)
        rs_s.append(u1); sh_s.append(u2); k_s.append(u3); v_s.append(u4); lf_s.append(u5)
    y_prompt, y_sample = hp, hs
    rwkv_state_p, rwkv_shift_p = jnp.stack(rs_p), jnp.stack(sh_p)
    fox_k_p, fox_v_p, fox_logf_p = jnp.stack(k_p), jnp.stack(v_p), jnp.stack(lf_p)
    rwkv_state_s, rwkv_shift_s = jnp.stack(rs_s), jnp.stack(sh_s)
    fox_k_s, fox_v_s, fox_logf_s = jnp.stack(k_s), jnp.stack(v_s), jnp.stack(lf_s)
    return (y_prompt, y_sample, rwkv_state_p, rwkv_shift_p, fox_k_p, fox_v_p, fox_logf_p,
            rwkv_state_s, rwkv_shift_s, fox_k_s, fox_v_s, fox_logf_s)
```

```python
import functools

import jax
import jax.numpy as jnp
from jax import lax
from jax.experimental import pallas as pl
from jax.experimental.pallas import tpu as pltpu

F32 = jnp.float32
BF16 = jnp.bfloat16

D_MODEL = 1024
HEAD_DIM = 64
N_HEADS = 8
WIDTH = N_HEADS * HEAD_DIM
N_PAIRS = N_HEADS // 2
DECAY_LORA = 64
AAA_LORA = 64
GATE_LORA = 128
RWKV_COLS = 3 * WIDTH + DECAY_LORA + AAA_LORA + GATE_LORA
LORA_OFF = 3 * WIDTH
D_FF = 2816
NORM_EPS = 1e-6
GN_EPS = 64e-5
LANES = 128
SUBLANES = 8
SCAN_CHUNK = LANES
VMEM_LIMIT = 56 * 1024 * 1024


def _cparams(sem):
    return pltpu.CompilerParams(dimension_semantics=sem, vmem_limit_bytes=VMEM_LIMIT)


def _const_spec(shape):
    nd = len(shape)
    return pl.BlockSpec(shape, lambda *_: (0,) * nd, pipeline_mode=pl.Buffered(1))


def _softplus(x):
    return jnp.maximum(x, 0.0) + jnp.log1p(jnp.exp(-jnp.abs(x)))


def _silu(x):
    return x * jax.nn.sigmoid(x)


def _dot(a, b):
    return jnp.dot(a, b, preferred_element_type=F32)


def _split2_dot(x, m):
    hi = x.astype(BF16)
    lo = (x - hi.astype(F32)).astype(BF16)
    return _dot(hi, m) + _dot(lo, m)


def _split3_dot_rhs(m, x):
    hi = x.astype(BF16)
    r1 = x - hi.astype(F32)
    mid = r1.astype(BF16)
    lo = (r1 - mid.astype(F32)).astype(BF16)
    return _dot(m, hi) + _dot(m, mid) + _dot(m, lo)


def _split3_dot_lhs(x, m):
    hi = x.astype(BF16)
    r1 = x - hi.astype(F32)
    mid = r1.astype(BF16)
    lo = (r1 - mid.astype(F32)).astype(BF16)
    return _dot(hi, m) + _dot(mid, m) + _dot(lo, m)


def _mod_kernel(c_ref, w_ref, b_ref, o_ref):
    o_ref[...] = _dot(_silu(c_ref[...]).astype(BF16), w_ref[...]) + b_ref[...]


def _adaln(c_all, w_ada_bf, b_ada):
    rows = c_all.shape[0]
    n = w_ada_bf.shape[1]
    bn = D_MODEL
    return pl.pallas_call(
        _mod_kernel,
        out_shape=jax.ShapeDtypeStruct((rows, n), F32),
        grid=(n // bn,),
        in_specs=[pl.BlockSpec((rows, D_MODEL), lambda j: (0, 0)),
                  pl.BlockSpec((D_MODEL, bn), lambda j: (0, j)),
                  pl.BlockSpec((1, bn), lambda j: (0, j))],
        out_specs=pl.BlockSpec((rows, bn), lambda j: (0, j)),
        compiler_params=_cparams(("arbitrary",)),
        name="adaln_mod",
    )(c_all, w_ada_bf, b_ada)


def _inproj_kernel(x_ref, mod_ref, g1_ref, wr_ref, wf_ref, wfl_ref, sprev_ref, mu_ref, w0_ref, a0_ref,
                   wdu_ref, wau_ref, wgu_ref, kk_ref, ka_ref, rk_ref, qg_ref, kg_ref, fb_ref, bd_ref,
                   r_o, w_o, k_o, v_o, kk_o, b_o, g_o, bonus_o, qb_o, kb_o, vb_o, kf_o, vf_o, lf_o, sh_o,
                   carry_ref, *, tm):
    t = pl.program_id(1)
    bd = bd_ref[...]
    x = x_ref[0]
    sh1 = mod_ref[0, 0:1, :]
    sc1 = mod_ref[0, 1:2, :]
    h = x * lax.rsqrt(jnp.mean(x * x, axis=-1, keepdims=True) + NORM_EPS) * g1_ref[...]
    hb = (h * (1.0 + sc1) + sh1).astype(BF16)
    pr = _dot(hb, wr_ref[...])
    pf = _dot(hb, wf_ref[...])
    pfl = _dot(hb, wfl_ref[...])

    @pl.when(t == 0)
    def _():
        carry_ref[...] = sprev_ref[0]

    prev = pltpu.roll(pr, 1, axis=0)
    first_row = lax.broadcasted_iota(jnp.int32, pr.shape, 0) == 0
    prev = jnp.where(first_row, carry_ref[...], prev)
    last = pr[tm - 1:tm, :]
    carry_ref[...] = last
    sh_o[0] = last
    z = pr + (prev - pr) * mu_ref[...]

    r = z[:, 0:WIDTH]
    k = z[:, WIDTH:2 * WIDTH]
    v = z[:, 2 * WIDTH:3 * WIDTH]
    dwa = z[:, LORA_OFF:LORA_OFF + LANES]
    dg = z[:, LORA_OFF + LANES:RWKV_COLS]
    w_log = -_softplus(-(w0_ref[...] + _dot(jnp.tanh(dwa).astype(BF16), wdu_ref[...]))) - 0.5
    decay = jnp.exp(-jnp.exp(w_log))
    a = jax.nn.sigmoid(a0_ref[...] + _dot(dwa.astype(BF16), wau_ref[...]))
    g = _dot(jax.nn.sigmoid(dg).astype(BF16), wgu_ref[...])
    kk = k * kk_ref[...]
    kk = kk / jnp.maximum(jnp.sqrt(_split2_dot(kk * kk, bd)), 1e-12)
    k2 = k * (1.0 + (a - 1.0) * ka_ref[...])
    bonus = _split2_dot(r * k2 * rk_ref[...], bd) * v

    r_o[0] = r
    w_o[0] = decay
    k_o[0] = k2
    v_o[0] = v
    kk_o[0] = kk
    b_o[0] = kk * a
    g_o[0] = g
    bonus_o[0] = bonus

    q = pf[:, 0:WIDTH]
    kf = pf[:, WIDTH:2 * WIDTH]
    vf = pf[:, 2 * WIDTH:3 * WIDTH]
    inv_hd = 1.0 / HEAD_DIM
    qn = q * lax.rsqrt(_split2_dot(q * q, bd) * inv_hd + NORM_EPS) * qg_ref[...]
    kn = kf * lax.rsqrt(_split2_dot(kf * kf, bd) * inv_hd + NORM_EPS) * kg_ref[...]
    qb_o[0] = (qn * (HEAD_DIM ** -0.5)).astype(BF16)
    kb_o[0] = kn.astype(BF16)
    vb_o[0] = vf.astype(BF16)
    kf_o[0] = kn
    vf_o[0] = vf
    lf_o[0] = -_softplus(-(pfl + fb_ref[...]))


def _inproj(x, modp, shift_prev, wts, tm):
    b, t, _ = x.shape
    nt = t // tm
    tok = lambda w, dt: jax.ShapeDtypeStruct((b, t, w), dt)
    tile = lambda w: pl.BlockSpec((1, tm, w), lambda i, j: (i, j, 0))
    consts = [wts[n] for n in ("norm1_g", "w_r", "w_f", "w_fl")]
    consts2 = [wts[n] for n in ("shift_mu", "w0", "a0", "w_du", "w_au", "w_gu", "k_k", "k_a", "r_k",
                                "fox_q_g", "fox_k_g", "fox_f_b", "bd")]
    in_specs = ([tile(D_MODEL), pl.BlockSpec((1, 6, D_MODEL), lambda i, j: (i, 0, 0))]
                + [_const_spec(c.shape) for c in consts]
                + [pl.BlockSpec((1, 1, RWKV_COLS), lambda i, j: (i, 0, 0))]
                + [_const_spec(c.shape) for c in consts2])
    out_shape = ([tok(WIDTH, F32)] * 8 + [tok(WIDTH, BF16)] * 3 + [tok(WIDTH, F32)] * 2
                 + [tok(LANES, F32), jax.ShapeDtypeStruct((b, 1, RWKV_COLS), F32)])
    out_specs = ([tile(WIDTH)] * 13 + [tile(LANES), pl.BlockSpec((1, 1, RWKV_COLS), lambda i, j: (i, 0, 0))])
    return pl.pallas_call(
        functools.partial(_inproj_kernel, tm=tm),
        out_shape=out_shape,
        grid=(b, nt),
        in_specs=in_specs,
        out_specs=out_specs,
        scratch_shapes=[pltpu.VMEM((1, RWKV_COLS), F32)],
        compiler_params=_cparams(("arbitrary", "arbitrary")),
        name="inproj",
    )(x, modp, *consts, shift_prev, *consts2)


def _scan_kernel(r_ref, w_ref, k_ref, v_ref, kk_ref, b_ref, s0_ref, y_ref, s_ref, vt_ref, yt_ref, *, n_steps):
    t = pl.program_id(1)

    @pl.when(t == 0)
    def _():
        s_ref[...] = s0_ref[...]

    vt_ref[...] = v_ref[0].T
    yt_ref[...] = jnp.zeros_like(yt_ref)
    lane = lax.broadcasted_iota(jnp.int32, (HEAD_DIM, LANES), 1)
    lo = lane < HEAD_DIM

    def half_sums(m):
        sa = jnp.sum(jnp.where(lo, m, 0.0), axis=-1, keepdims=True)
        sb = jnp.sum(jnp.where(lo, 0.0, m), axis=-1, keepdims=True)
        return sa, sb

    def body(j, states):
        base = pl.multiple_of(j * SUBLANES, SUBLANES)
        states = list(states)
        for p in range(N_PAIRS):
            cols = pl.ds(p * LANES, LANES)
            rows8 = lambda ref: ref[0, pl.ds(base, SUBLANES), cols]
            kk8, w8, b8, k8, r8 = (rows8(ref) for ref in (kk_ref, w_ref, b_ref, k_ref, r_ref))
            ra = p * LANES
            rb = p * LANES + HEAD_DIM
            s = states[p]
            for u in range(SUBLANES):
                tok = lane == base + u
                row = lambda blk: blk[u:u + 1, :]
                sa_a, sa_b = half_sums(s * row(kk8))
                sa = jnp.where(lo, sa_a, sa_b)
                v_a = jnp.sum(jnp.where(tok, vt_ref[ra:ra + HEAD_DIM, :], 0.0), axis=-1, keepdims=True)
                v_b = jnp.sum(jnp.where(tok, vt_ref[rb:rb + HEAD_DIM, :], 0.0), axis=-1, keepdims=True)
                vcol = jnp.where(lo, v_a, v_b)
                s = s * row(w8) - sa * row(b8) + vcol * row(k8)
                y_a, y_b = half_sums(s * row(r8))
                yt_ref[ra:ra + HEAD_DIM, :] = jnp.where(tok, y_a, yt_ref[ra:ra + HEAD_DIM, :])
                yt_ref[rb:rb + HEAD_DIM, :] = jnp.where(tok, y_b, yt_ref[rb:rb + HEAD_DIM, :])
            states[p] = s
        return tuple(states)

    init = tuple(s_ref[0, p] for p in range(N_PAIRS))
    final = lax.fori_loop(0, n_steps // SUBLANES, body, init)
    for p in range(N_PAIRS):
        s_ref[0, p] = final[p]
    y_ref[0] = yt_ref[...].T


def _rwkv_scan(r, w, k, v, kk, bvec, s0_pairs, n_valid):
    b, t, _ = r.shape
    nt = t // SCAN_CHUNK
    n_steps = n_valid - (nt - 1) * SCAN_CHUNK
    assert (nt == 1 or n_steps == SCAN_CHUNK) and n_steps % SUBLANES == 0
    tile = pl.BlockSpec((1, SCAN_CHUNK, WIDTH), lambda i, j: (i, j, 0))
    st = pl.BlockSpec((1, N_PAIRS, HEAD_DIM, LANES), lambda i, j: (i, 0, 0, 0))
    return pl.pallas_call(
        functools.partial(_scan_kernel, n_steps=n_steps),
        out_shape=[jax.ShapeDtypeStruct((b, t, WIDTH), F32),
                   jax.ShapeDtypeStruct((b, N_PAIRS, HEAD_DIM, LANES), F32)],
        grid=(b, nt),
        in_specs=[tile] * 6 + [st],
        out_specs=[tile, st],
        scratch_shapes=[pltpu.VMEM((WIDTH, SCAN_CHUNK), F32), pltpu.VMEM((WIDTH, SCAN_CHUNK), F32)],
        compiler_params=_cparams(("arbitrary", "arbitrary")),
        name="rwkv_scan",
    )(r, w, k, v, kk, bvec, s0_pairs)


def _cumsum_kernel(x_ref, u_ref, l_ref, o_ref):
    x = x_ref[0]
    local = _split3_dot_lhs(x, u_ref[...])
    tot = jnp.broadcast_to(local[:, LANES - 1:LANES], local.shape)
    o_ref[0] = local + _split3_dot_rhs(l_ref[...], tot)


def _cumsum(logf_t):
    b, h, t = logf_t.shape
    nb = t // LANES
    rows = h * nb
    x = logf_t.reshape(b, rows, LANES)
    col = jnp.arange(LANES)
    u = (col[:, None] <= col[None, :]).astype(BF16)
    rid = jnp.arange(rows)
    l = ((rid[:, None] // nb == rid[None, :] // nb) & (rid[None, :] < rid[:, None])).astype(BF16)
    out = pl.pallas_call(
        _cumsum_kernel,
        out_shape=jax.ShapeDtypeStruct((b, rows, LANES), F32),
        grid=(b,),
        in_specs=[pl.BlockSpec((1, rows, LANES), lambda i: (i, 0, 0)),
                  _const_spec(u.shape), _const_spec(l.shape)],
        out_specs=pl.BlockSpec((1, rows, LANES), lambda i: (i, 0, 0)),
        compiler_params=_cparams(("arbitrary",)),
        name="logf_cumsum",
    )(x, u, l)
    return out.reshape(b, h, t)


def _attn_kernel(q_ref, k_ref, v_ref, lcq_ref, lck_ref, o_ref, m_ref, l_ref, acc_ref, *, tq, tk, past, nk):
    qi = pl.program_id(2)
    ki = pl.program_id(3)

    @pl.when(ki == 0)
    def _():
        m_ref[...] = jnp.full_like(m_ref, -jnp.inf)
        l_ref[...] = jnp.zeros_like(l_ref)
        acc_ref[...] = jnp.zeros_like(acc_ref)

    q_first = past + qi * tq
    lane = lax.broadcasted_iota(jnp.int32, (1, LANES), 1)
    lo = lane < HEAD_DIM

    @pl.when(ki * tk <= q_first + tq - 1)
    def _():
        q2 = q_ref[0]
        k2 = k_ref[0]
        v2 = v_ref[0]
        q_pos = q_first + lax.broadcasted_iota(jnp.int32, (tq, tk), 0)
        k_pos = ki * tk + lax.broadcasted_iota(jnp.int32, (tq, tk), 1)
        causal = k_pos <= q_pos
        zero = jnp.zeros((), BF16)
        pv = []
        alphas = []
        for hb in range(2):
            mine = lo if hb == 0 else jnp.logical_not(lo)
            qh = jnp.where(mine, q2, zero)
            s = lax.dot_general(qh, k2, (((1,), (1,)), ((), ())), preferred_element_type=F32)
            s = s + lcq_ref[0, 0, :, hb:hb + 1] - lck_ref[0, 0, hb:hb + 1, :]
            s = jnp.where(causal, s, -jnp.inf)
            m_prev = m_ref[hb]
            m_new = jnp.maximum(m_prev, jnp.max(s, axis=-1, keepdims=True))
            alpha = jnp.exp(m_prev - m_new)
            p = jnp.exp(s - m_new)
            l_ref[hb] = alpha * l_ref[hb] + jnp.sum(p, axis=-1, keepdims=True)
            m_ref[hb] = m_new
            pv.append(_dot(p.astype(BF16), jnp.where(mine, v2, zero)))
            alphas.append(alpha)
        acc_ref[...] = acc_ref[...] * jnp.where(lo, alphas[0], alphas[1]) + pv[0] + pv[1]

    @pl.when(ki == nk - 1)
    def _():
        o_ref[0] = (acc_ref[...] / jnp.where(lo, l_ref[0], l_ref[1])).astype(o_ref.dtype)


def _fox_attention(qb, kb, vb, lcq, lck, past, tq, tk):
    b, t_q, _ = qb.shape
    t_k = kb.shape[1]
    nq, nk = t_q // tq, t_k // tk
    last_kv = lambda qi: (past + qi * tq + tq - 1) // tk
    kv_spec = pl.BlockSpec((1, tk, LANES), lambda i, p, qi, ki: (i, jnp.minimum(ki, last_kv(qi)), p))
    return pl.pallas_call(
        functools.partial(_attn_kernel, tq=tq, tk=tk, past=past, nk=nk),
        out_shape=jax.ShapeDtypeStruct((b, t_q, WIDTH), BF16),
        grid=(b, N_PAIRS, nq, nk),
        in_specs=[pl.BlockSpec((1, tq, LANES), lambda i, p, qi, ki: (i, qi, p)),
                  kv_spec, kv_spec,
                  pl.BlockSpec((1, 1, tq, 2), lambda i, p, qi, ki: (i, p, qi, 0)),
                  pl.BlockSpec((1, 1, 2, tk), lambda i, p, qi, ki: (i, p, 0, jnp.minimum(ki, last_kv(qi))))],
        out_specs=pl.BlockSpec((1, tq, LANES), lambda i, p, qi, ki: (i, qi, p)),
        scratch_shapes=[pltpu.VMEM((2, tq, 1), F32), pltpu.VMEM((2, tq, 1), F32), pltpu.VMEM((tq, LANES), F32)],
        compiler_params=_cparams(("arbitrary", "arbitrary", "arbitrary", "arbitrary")),
        name="fox_attention",
    )(qb, kb, vb, lcq, lck)


def _out_kernel(x_ref, ys_ref, g_ref, bonus_ref, yf_ref, mod_ref, gng_ref, gnb_ref, bd_ref, wo_ref, n2_ref,
                wg_ref, wu_ref, wd_ref, o_ref):
    bd = bd_ref[...]
    gt1 = mod_ref[0, 2:3, :]
    sh2 = mod_ref[0, 3:4, :]
    sc2 = mod_ref[0, 4:5, :]
    gt2 = mod_ref[0, 5:6, :]
    inv_hd = 1.0 / HEAD_DIM
    ys = ys_ref[0]
    d = ys - _split2_dot(ys, bd) * inv_hd
    var = _split2_dot(d * d, bd) * inv_hd
    yn = d * lax.rsqrt(var + GN_EPS) * gng_ref[...] + gnb_ref[...]
    y_rwkv = ((yn + bonus_ref[0]) * g_ref[0]).astype(BF16)
    mix = _dot(y_rwkv, wo_ref[0:WIDTH, :]) + _dot(yf_ref[0], wo_ref[WIDTH:2 * WIDTH, :])
    x1 = x_ref[0] + gt1 * mix
    h2 = x1 * lax.rsqrt(jnp.mean(x1 * x1, axis=-1, keepdims=True) + NORM_EPS) * n2_ref[...]
    hb = (h2 * (1.0 + sc2) + sh2).astype(BF16)
    act = (_silu(_dot(hb, wg_ref[...])) * _dot(hb, wu_ref[...])).astype(BF16)
    o_ref[0] = x1 + gt2 * _dot(act, wd_ref[...])


def _outproj_ffn(x, ys, g, bonus, yf, modp, wts, tm):
    b, t, _ = x.shape
    tile = lambda w: pl.BlockSpec((1, tm, w), lambda i, j: (i, j, 0))
    consts = [wts[n] for n in ("gn_g", "gn_b", "bd", "w_out", "norm2_g", "w_ffn_gate", "w_ffn_up", "w_ffn_down")]
    return pl.pallas_call(
        _out_kernel,
        out_shape=jax.ShapeDtypeStruct((b, t, D_MODEL), F32),
        grid=(b, t // tm),
        in_specs=([tile(D_MODEL), tile(WIDTH), tile(WIDTH), tile(WIDTH), tile(WIDTH),
                   pl.BlockSpec((1, 6, D_MODEL), lambda i, j: (i, 0, 0))]
                  + [_const_spec(c.shape) for c in consts]),
        out_specs=tile(D_MODEL),
        compiler_params=_cparams(("arbitrary", "arbitrary")),
        name="outproj_ffn",
    )(x, ys, g, bonus, yf, modp, *consts)


def _prep_weights(norm1_g, w_in, shift_mu, w0, w_decay_up, a0, w_aaa_up, w_gate_up, k_k, k_a, r_k, gn_g, gn_b,
                  fox_q_g, fox_k_g, fox_f_b, w_out, norm2_g, w_ffn_gate, w_ffn_up, w_ffn_down):
    row = lambda a: a.reshape(1, -1).astype(F32)
    fox0 = RWKV_COLS
    zeros_lora = jnp.zeros((DECAY_LORA, WIDTH), BF16)
    head = jnp.arange(WIDTH) // HEAD_DIM
    return {
        "norm1_g": row(norm1_g),
        "w_r": w_in[:, :RWKV_COLS].astype(BF16),
        "w_f": w_in[:, fox0:fox0 + 3 * WIDTH].astype(BF16),
        "w_fl": jnp.pad(w_in[:, fox0 + 3 * WIDTH:], ((0, 0), (0, LANES - N_HEADS))).astype(BF16),
        "shift_mu": row(shift_mu), "w0": row(w0), "a0": row(a0),
        "w_du": jnp.concatenate([w_decay_up.astype(BF16), zeros_lora], axis=0),
        "w_au": jnp.concatenate([zeros_lora, w_aaa_up.astype(BF16)], axis=0),
        "w_gu": w_gate_up.astype(BF16),
        "k_k": row(k_k), "k_a": row(k_a), "r_k": row(r_k),
        "fox_q_g": row(jnp.tile(fox_q_g, N_HEADS)), "fox_k_g": row(jnp.tile(fox_k_g, N_HEADS)),
        "fox_f_b": jnp.pad(row(fox_f_b), ((0, 0), (0, LANES - N_HEADS))),
        "bd": (head[:, None] == head[None, :]).astype(BF16),
        "gn_g": row(gn_g), "gn_b": row(gn_b),
        "w_out": w_out.astype(BF16), "norm2_g": row(norm2_g),
        "w_ffn_gate": w_ffn_gate.astype(BF16), "w_ffn_up": w_ffn_up.astype(BF16),
        "w_ffn_down": w_ffn_down.astype(BF16),
    }


def _to_pairs(state):
    b = state.shape[0]
    s = state.reshape(b, N_PAIRS, 2, HEAD_DIM, HEAD_DIM)
    return jnp.transpose(s, (0, 1, 3, 2, 4)).reshape(b, N_PAIRS, HEAD_DIM, LANES)


def _from_pairs(state):
    b = state.shape[0]
    s = state.reshape(b, N_PAIRS, HEAD_DIM, 2, HEAD_DIM)
    return jnp.transpose(s, (0, 1, 3, 2, 4)).reshape(b, N_HEADS, HEAD_DIM, HEAD_DIM)


def _round_up(n, m):
    return -(-n // m) * m


def _layer(x, modp, shift_prev, s_prev, k_past, v_past, logf_past, wts, tm, tq):
    b, t, _ = x.shape
    past = k_past.shape[1]
    (r, w, k2, v, kk, bvec, g, bonus, qb, kb, vb, kf, vf, lf_pad, new_shift) = _inproj(x, modp, shift_prev, wts, tm)

    t_pad = _round_up(t, SCAN_CHUNK)
    padt = lambda a: jnp.pad(a, ((0, 0), (0, t_pad - t), (0, 0))) if t_pad != t else a
    ys, s_new = _rwkv_scan(*(padt(a) for a in (r, w, k2, v, kk, bvec)), _to_pairs(s_prev.astype(F32)), t)
    ys = ys[:, :t]

    logf = lf_pad[:, :, :N_HEADS]
    t_all = past + t
    tk_all = _round_up(t_all, LANES)
    lf_all = jnp.concatenate([logf_past.astype(F32), logf], axis=1)
    lf_all = jnp.pad(lf_all, ((0, 0), (0, tk_all - t_all), (0, 0)))
    lc = _cumsum(jnp.transpose(lf_all, (0, 2, 1)))
    lc_pairs = lc.reshape(b, N_PAIRS, 2, tk_all)
    lcq = jnp.transpose(lc_pairs[:, :, :, past:past + t], (0, 1, 3, 2))
    kv_pad = ((0, 0), (0, tk_all - t_all), (0, 0))
    k_all = jnp.pad(jnp.concatenate([k_past.reshape(b, past, WIDTH).astype(BF16), kb], axis=1), kv_pad)
    v_all = jnp.pad(jnp.concatenate([v_past.reshape(b, past, WIDTH).astype(BF16), vb], axis=1), kv_pad)
    tk = tq if tk_all % tq == 0 and t > tq else tk_all
    yf = _fox_attention(qb, k_all, v_all, lcq, lc_pairs, past, min(tq, t), tk)

    y = _outproj_ffn(x, ys, g, bonus, yf, modp, wts, tm)
    return (y, _from_pairs(s_new), new_shift, kf.reshape(b, t, N_HEADS, HEAD_DIM),
            vf.reshape(b, t, N_HEADS, HEAD_DIM), logf)


def kernel(x_prompt, x_sample, cache_fox_k, cache_fox_v, cache_fox_logf, state_rwkv, state_rwkv_shift, c_prompt, c_sample, norm1_g, w_ada, b_ada, w_in, shift_mu, w0, w_decay_up, a0, w_aaa_up, w_gate_up, k_k, k_a, r_k, gn_g, gn_b, fox_q_g, fox_k_g, fox_f_b, w_out, norm2_g, w_ffn_gate, w_ffn_up, w_ffn_down):
    depth = w_in.shape[0]
    bp, bs = x_prompt.shape[0], x_sample.shape[0]
    zero_shift = jnp.zeros((bp, 1, RWKV_COLS), F32)
    zero_state = jnp.zeros((bp, N_HEADS, HEAD_DIM, HEAD_DIM), F32)
    zero_kv = jnp.zeros((bp, 0, N_HEADS, HEAD_DIM), F32)
    zero_logf = jnp.zeros((bp, 0, N_HEADS), F32)
    hp, hs = x_prompt, x_sample
    outs_p, outs_s = [], []
    for l in range(depth):
        wts = _prep_weights(norm1_g[l], w_in[l], shift_mu[l], w0[l], w_decay_up[l], a0[l], w_aaa_up[l],
                            w_gate_up[l], k_k[l], k_a[l], r_k[l], gn_g[l], gn_b[l], fox_q_g[l], fox_k_g[l],
                            fox_f_b[l], w_out[l], norm2_g[l], w_ffn_gate[l], w_ffn_up[l], w_ffn_down[l])
        rows = _round_up(bp + bs, 8)
        c_all = jnp.pad(jnp.concatenate([c_prompt, c_sample], axis=0), ((0, rows - bp - bs), (0, 0)))
        mod = _adaln(c_all, w_ada[l].astype(BF16), b_ada[l].reshape(1, -1)).reshape(rows, 6, D_MODEL)
        res_p = _layer(hp, mod[:bp], zero_shift, zero_state, zero_kv, zero_kv, zero_logf, wts, tm=256, tq=512)
        res_s = _layer(hs, mod[bp:bp + bs], state_rwkv_shift[l], state_rwkv[l], cache_fox_k[l], cache_fox_v[l],
                       cache_fox_logf[l], wts, tm=64, tq=512)
        hp, hs = res_p[0], res_s[0]
        outs_p.append(res_p[1:])
        outs_s.append(res_s[1:])
    stack = lambda outs, i: jnp.stack([o[i] for o in outs])
    return (hp, hs,
            stack(outs_p, 0), stack(outs_p, 1), stack(outs_p, 2), stack(outs_p, 3), stack(outs_p, 4),
            stack(outs_s, 0), stack(outs_s, 1), stack(outs_s, 2), stack(outs_s, 3), stack(outs_s, 4))
```

```python
import functools

import jax
import jax.numpy as jnp
from jax import lax
from jax.experimental import pallas as pl
from jax.experimental.pallas import tpu as pltpu

F32 = jnp.float32
BF16 = jnp.bfloat16

D_MODEL = 1024
HEAD_DIM = 64
N_HEADS = 8
WIDTH = N_HEADS * HEAD_DIM
N_PAIRS = N_HEADS // 2
DECAY_LORA = 64
AAA_LORA = 64
GATE_LORA = 128
RWKV_COLS = 3 * WIDTH + DECAY_LORA + AAA_LORA + GATE_LORA
LORA_OFF = 3 * WIDTH
D_FF = 2816
NORM_EPS = 1e-6
GN_EPS = 64e-5
LANES = 128
SUBLANES = 8
SCAN_CHUNK = 512
VMEM_LIMIT = 56 * 1024 * 1024


def _cparams(sem):
    return pltpu.CompilerParams(dimension_semantics=sem, vmem_limit_bytes=VMEM_LIMIT)


def _const_spec(shape):
    nd = len(shape)
    return pl.BlockSpec(shape, lambda *_: (0,) * nd, pipeline_mode=pl.Buffered(1))


def _softplus(x):
    return jnp.maximum(x, 0.0) + jnp.log1p(jnp.exp(-jnp.abs(x)))


def _silu(x):
    return x * jax.nn.sigmoid(x)


def _dot(a, b):
    return jnp.dot(a, b, preferred_element_type=F32)


def _split2_dot(x, m):
    hi = x.astype(BF16)
    lo = (x - hi.astype(F32)).astype(BF16)
    return _dot(hi, m) + _dot(lo, m)


def _split3_dot_rhs(m, x):
    hi = x.astype(BF16)
    r1 = x - hi.astype(F32)
    mid = r1.astype(BF16)
    lo = (r1 - mid.astype(F32)).astype(BF16)
    return _dot(m, hi) + _dot(m, mid) + _dot(m, lo)


def _split3_dot_lhs(x, m):
    hi = x.astype(BF16)
    r1 = x - hi.astype(F32)
    mid = r1.astype(BF16)
    lo = (r1 - mid.astype(F32)).astype(BF16)
    return _dot(hi, m) + _dot(mid, m) + _dot(lo, m)


def _mod_kernel(c_ref, w_ref, b_ref, o_ref):
    o_ref[...] = _dot(_silu(c_ref[...]).astype(BF16), w_ref[...]) + b_ref[...]


def _adaln(c_all, w_ada_bf, b_ada):
    rows = c_all.shape[0]
    n = w_ada_bf.shape[1]
    bn = D_MODEL
    return pl.pallas_call(
        _mod_kernel,
        out_shape=jax.ShapeDtypeStruct((rows, n), F32),
        grid=(n // bn,),
        in_specs=[pl.BlockSpec((rows, D_MODEL), lambda j: (0, 0)),
                  pl.BlockSpec((D_MODEL, bn), lambda j: (0, j)),
                  pl.BlockSpec((1, bn), lambda j: (0, j))],
        out_specs=pl.BlockSpec((rows, bn), lambda j: (0, j)),
        compiler_params=_cparams(("arbitrary",)),
        name="adaln_mod",
    )(c_all, w_ada_bf, b_ada)


def _inproj_kernel(x_ref, mod_ref, g1_ref, wr_ref, wf_ref, wfl_ref, sprev_ref, mu_ref, w0_ref, a0_ref,
                   wdu_ref, wau_ref, wgu_ref, kk_ref, ka_ref, rk_ref, qg_ref, kg_ref, fb_ref, bd_ref,
                   r_o, w_o, k_o, v_o, kk_o, b_o, g_o, bonus_o, qb_o, kb_o, vb_o, kf_o, vf_o, lf_o, sh_o,
                   carry_ref, *, tm):
    t = pl.program_id(1)
    bd = bd_ref[...]
    x = x_ref[0]
    sh1 = mod_ref[0, 0:1, :]
    sc1 = mod_ref[0, 1:2, :]
    h = x * lax.rsqrt(jnp.mean(x * x, axis=-1, keepdims=True) + NORM_EPS) * g1_ref[...]
    hb = (h * (1.0 + sc1) + sh1).astype(BF16)
    pr = _dot(hb, wr_ref[...])
    pf = _dot(hb, wf_ref[...])
    pfl = _dot(hb, wfl_ref[...])

    @pl.when(t == 0)
    def _():
        carry_ref[...] = sprev_ref[0]

    prev = pltpu.roll(pr, 1, axis=0)
    first_row = lax.broadcasted_iota(jnp.int32, pr.shape, 0) == 0
    prev = jnp.where(first_row, carry_ref[...], prev)
    last = pr[tm - 1:tm, :]
    carry_ref[...] = last
    sh_o[0] = last
    z = pr + (prev - pr) * mu_ref[...]

    r = z[:, 0:WIDTH]
    k = z[:, WIDTH:2 * WIDTH]
    v = z[:, 2 * WIDTH:3 * WIDTH]
    dwa = z[:, LORA_OFF:LORA_OFF + LANES]
    dg = z[:, LORA_OFF + LANES:RWKV_COLS]
    w_log = -_softplus(-(w0_ref[...] + _dot(jnp.tanh(dwa).astype(BF16), wdu_ref[...]))) - 0.5
    decay = jnp.exp(-jnp.exp(w_log))
    a = jax.nn.sigmoid(a0_ref[...] + _dot(dwa.astype(BF16), wau_ref[...]))
    g = _dot(jax.nn.sigmoid(dg).astype(BF16), wgu_ref[...])
    kk = k * kk_ref[...]
    kk = kk / jnp.maximum(jnp.sqrt(_split2_dot(kk * kk, bd)), 1e-12)
    k2 = k * (1.0 + (a - 1.0) * ka_ref[...])
    bonus = _split2_dot(r * k2 * rk_ref[...], bd) * v

    r_o[0] = r
    w_o[0] = decay
    k_o[0] = k2
    v_o[0] = v
    kk_o[0] = kk
    b_o[0] = kk * a
    g_o[0] = g
    bonus_o[0] = bonus

    q = pf[:, 0:WIDTH]
    kf = pf[:, WIDTH:2 * WIDTH]
    vf = pf[:, 2 * WIDTH:3 * WIDTH]
    inv_hd = 1.0 / HEAD_DIM
    qn = q * lax.rsqrt(_split2_dot(q * q, bd) * inv_hd + NORM_EPS) * qg_ref[...]
    kn = kf * lax.rsqrt(_split2_dot(kf * kf, bd) * inv_hd + NORM_EPS) * kg_ref[...]
    qb_o[0] = (qn * (HEAD_DIM ** -0.5)).astype(BF16)
    kb_o[0] = kn.astype(BF16)
    vb_o[0] = vf.astype(BF16)
    kf_o[0] = kn
    vf_o[0] = vf
    lf_o[0] = -_softplus(-(pfl + fb_ref[...]))


def _inproj(x, modp, shift_prev, wts, tm):
    b, t, _ = x.shape
    nt = t // tm
    tok = lambda w, dt: jax.ShapeDtypeStruct((b, t, w), dt)
    tile = lambda w: pl.BlockSpec((1, tm, w), lambda i, j: (i, j, 0))
    consts = [wts[n] for n in ("norm1_g", "w_r", "w_f", "w_fl")]
    consts2 = [wts[n] for n in ("shift_mu", "w0", "a0", "w_du", "w_au", "w_gu", "k_k", "k_a", "r_k",
                                "fox_q_g", "fox_k_g", "fox_f_b", "bd")]
    in_specs = ([tile(D_MODEL), pl.BlockSpec((1, 6, D_MODEL), lambda i, j: (i, 0, 0))]
                + [_const_spec(c.shape) for c in consts]
                + [pl.BlockSpec((1, 1, RWKV_COLS), lambda i, j: (i, 0, 0))]
                + [_const_spec(c.shape) for c in consts2])
    out_shape = ([tok(WIDTH, F32)] * 8 + [tok(WIDTH, BF16)] * 3 + [tok(WIDTH, F32)] * 2
                 + [tok(LANES, F32), jax.ShapeDtypeStruct((b, 1, RWKV_COLS), F32)])
    out_specs = ([tile(WIDTH)] * 13 + [tile(LANES), pl.BlockSpec((1, 1, RWKV_COLS), lambda i, j: (i, 0, 0))])
    return pl.pallas_call(
        functools.partial(_inproj_kernel, tm=tm),
        out_shape=out_shape,
        grid=(b, nt),
        in_specs=in_specs,
        out_specs=out_specs,
        scratch_shapes=[pltpu.VMEM((1, RWKV_COLS), F32)],
        compiler_params=_cparams(("arbitrary", "arbitrary")),
        name="inproj",
    )(x, modp, *consts, shift_prev, *consts2)


def _exact_pieces(x):
    hi = x.astype(BF16).astype(F32)
    r1 = x - hi
    mid = r1.astype(BF16).astype(F32)
    return hi, mid, r1 - mid


def _scan_kernel(r_ref, w_ref, k_ref, v_ref, kk_ref, b_ref, s0_ref, sel_ref, e_ref, y_ref, s_ref,
                 tile_ref, gam_ref, *, n_steps):
    t = pl.program_id(1)

    @pl.when(t == 0)
    def _():
        s_ref[...] = s0_ref[...]

    sel = sel_ref[...]
    e_all = e_ref[...]
    row_id = lax.broadcasted_iota(jnp.int32, (SUBLANES, LANES), 0)
    lo8 = lax.broadcasted_iota(jnp.int32, (SUBLANES, LANES), 1) < HEAD_DIM
    q_pad = jnp.zeros((LANES - 6 * SUBLANES, LANES), F32)

    def transposed_halves(x8):
        blocks = []
        for h in range(2):
            blocks.extend(_exact_pieces(jnp.where(lo8 if h == 0 else jnp.logical_not(lo8), x8, 0.0)))
        q = jnp.concatenate(blocks + [q_pad], axis=0).astype(BF16)
        return lax.dot_general(sel, q, (((1,), (1,)), ((), ())), preferred_element_type=F32).astype(BF16)

    def shift_rows(x, sh, fill):
        return jnp.where(row_id >= sh, pltpu.roll(x, sh, axis=0), fill)

    n_groups = n_steps // SUBLANES
    last_lanes = (SUBLANES - 1) * LANES

    def make_tiles(group, slot, p):
        base = pl.multiple_of(group * SUBLANES, SUBLANES)
        rows8 = lambda ref: ref[0, pl.ds(base, SUBLANES), pl.ds(p * LANES, LANES)]
        g_inc = rows8(w_ref)
        for sh in (1, 2, 4):
            g_inc = g_inc * shift_rows(g_inc, sh, 1.0)
        g_exc = shift_rows(g_inc, 1, 1.0)
        inv = 1.0 / g_inc
        scaled = (rows8(kk_ref) * g_exc, rows8(b_ref) * inv, rows8(k_ref) * inv, rows8(r_ref) * g_inc)
        lhs = jnp.concatenate([transposed_halves(x) for x in scaled], axis=0)
        tile_ref[slot, p] = _dot(lhs, e_all)
        gam_ref[slot, p] = _dot(transposed_halves(g_inc), e_all[:, last_lanes:last_lanes + LANES])

    def run_tokens(group, slot, states, next_group):
        base = pl.multiple_of(group * SUBLANES, SUBLANES)
        states = list(states)
        v8 = [v_ref[0, pl.ds(base, SUBLANES), pl.ds(p * LANES, LANES)] for p in range(N_PAIRS)]
        ys = [[] for _ in range(N_PAIRS)]
        for u in range(SUBLANES):
            if u % 2 == 0:
                make_tiles(next_group, 1 - slot, u // 2)
            for p in range(N_PAIRS):
                col = lambda vec: tile_ref[slot, p, vec * HEAD_DIM:(vec + 1) * HEAD_DIM, u * LANES:(u + 1) * LANES]
                s = states[p]
                sa = jnp.sum(s * col(0), axis=0, keepdims=True)
                s = s + col(2) * v8[p][u:u + 1, :] - col(1) * sa
                ys[p].append(jnp.sum(s * col(3), axis=0, keepdims=True))
                states[p] = s
        for p in range(N_PAIRS):
            states[p] = states[p] * gam_ref[slot, p]
            y_ref[0, pl.ds(base, SUBLANES), pl.ds(p * LANES, LANES)] = jnp.concatenate(ys[p], axis=0)
        return tuple(states)

    def body(jj, states):
        g0 = 2 * jj
        states = run_tokens(g0, 0, states, g0 + 1)
        return run_tokens(g0 + 1, 1, states, jnp.minimum(g0 + 2, n_groups - 1))

    for p in range(N_PAIRS):
        make_tiles(0, 0, p)
    init = tuple(s_ref[0, p] for p in range(N_PAIRS))
    final = lax.fori_loop(0, n_groups // 2, body, init)
    for p in range(N_PAIRS):
        s_ref[0, p] = final[p]


def _rwkv_scan(r, w, k, v, kk, bvec, s0_pairs):
    b, t, _ = r.shape
    chunk = min(SCAN_CHUNK, t)
    assert t % chunk == 0 and chunk % SUBLANES == 0
    nt = t // chunk
    lane = jnp.arange(LANES)
    sel = (lane[None, :] % HEAD_DIM == jnp.arange(HEAD_DIM)[:, None]).astype(BF16)
    q_row = jnp.arange(LANES)
    q_half, q_tok, q_used = q_row // (3 * SUBLANES), q_row % SUBLANES, q_row < 6 * SUBLANES
    col = jnp.arange(SUBLANES * LANES)
    e_all = (q_used[:, None] & (q_tok[:, None] == col[None, :] // LANES)
             & (q_half[:, None] == (col[None, :] % LANES) // HEAD_DIM)).astype(BF16)
    tile = pl.BlockSpec((1, chunk, WIDTH), lambda i, j: (i, j, 0))
    st = pl.BlockSpec((1, N_PAIRS, HEAD_DIM, LANES), lambda i, j: (i, 0, 0, 0))
    return pl.pallas_call(
        functools.partial(_scan_kernel, n_steps=chunk),
        out_shape=[jax.ShapeDtypeStruct((b, t, WIDTH), F32),
                   jax.ShapeDtypeStruct((b, N_PAIRS, HEAD_DIM, LANES), F32)],
        grid=(b, nt),
        in_specs=[tile] * 6 + [st, _const_spec(sel.shape), _const_spec(e_all.shape)],
        out_specs=[tile, st],
        scratch_shapes=[pltpu.VMEM((2, N_PAIRS, 4 * HEAD_DIM, SUBLANES * LANES), F32),
                        pltpu.VMEM((2, N_PAIRS, HEAD_DIM, LANES), F32)],
        compiler_params=_cparams(("arbitrary", "arbitrary")),
        name="rwkv_scan",
    )(r, w, k, v, kk, bvec, s0_pairs, sel, e_all)


def _cumsum_kernel(x_ref, u_ref, l_ref, o_ref):
    x = x_ref[0]
    local = _split3_dot_lhs(x, u_ref[...])
    tot = jnp.broadcast_to(local[:, LANES - 1:LANES], local.shape)
    o_ref[0] = local + _split3_dot_rhs(l_ref[...], tot)


def _cumsum(logf_t):
    b, h, t = logf_t.shape
    nb = t // LANES
    rows = h * nb
    x = logf_t.reshape(b, rows, LANES)
    col = jnp.arange(LANES)
    u = (col[:, None] <= col[None, :]).astype(BF16)
    rid = jnp.arange(rows)
    l = ((rid[:, None] // nb == rid[None, :] // nb) & (rid[None, :] < rid[:, None])).astype(BF16)
    out = pl.pallas_call(
        _cumsum_kernel,
        out_shape=jax.ShapeDtypeStruct((b, rows, LANES), F32),
        grid=(b,),
        in_specs=[pl.BlockSpec((1, rows, LANES), lambda i: (i, 0, 0)),
                  _const_spec(u.shape), _const_spec(l.shape)],
        out_specs=pl.BlockSpec((1, rows, LANES), lambda i: (i, 0, 0)),
        compiler_params=_cparams(("arbitrary",)),
        name="logf_cumsum",
    )(x, u, l)
    return out.reshape(b, h, t)


def _attn_kernel(q_ref, k_ref, v_ref, lcq_ref, lck_ref, o_ref, m_ref, l_ref, acc_ref, *, tq, tk, past, nk):
    qi = pl.program_id(2)
    ki = pl.program_id(3)

    @pl.when(ki == 0)
    def _():
        m_ref[...] = jnp.full_like(m_ref, -jnp.inf)
        l_ref[...] = jnp.zeros_like(l_ref)
        acc_ref[...] = jnp.zeros_like(acc_ref)

    q_first = past + qi * tq
    lane = lax.broadcasted_iota(jnp.int32, (1, LANES), 1)
    lo = lane < HEAD_DIM

    @pl.when(ki * tk <= q_first + tq - 1)
    def _():
        q2 = q_ref[0]
        k2 = k_ref[0]
        v2 = v_ref[0]
        q_pos = q_first + lax.broadcasted_iota(jnp.int32, (tq, tk), 0)
        k_pos = ki * tk + lax.broadcasted_iota(jnp.int32, (tq, tk), 1)
        causal = k_pos <= q_pos
        zero = jnp.zeros((), BF16)
        pv = []
        alphas = []
        for hb in range(2):
            mine = lo if hb == 0 else jnp.logical_not(lo)
            qh = jnp.where(mine, q2, zero)
            s = lax.dot_general(qh, k2, (((1,), (1,)), ((), ())), preferred_element_type=F32)
            s = s + lcq_ref[0, 0, :, hb:hb + 1] - lck_ref[0, 0, hb:hb + 1, :]
            s = jnp.where(causal, s, -jnp.inf)
            m_prev = m_ref[hb]
            m_new = jnp.maximum(m_prev, jnp.max(s, axis=-1, keepdims=True))
            alpha = jnp.exp(m_prev - m_new)
            p = jnp.exp(s - m_new)
            l_ref[hb] = alpha * l_ref[hb] + jnp.sum(p, axis=-1, keepdims=True)
            m_ref[hb] = m_new
            pv.append(_dot(p.astype(BF16), jnp.where(mine, v2, zero)))
            alphas.append(alpha)
        acc_ref[...] = acc_ref[...] * jnp.where(lo, alphas[0], alphas[1]) + pv[0] + pv[1]

    @pl.when(ki == nk - 1)
    def _():
        o_ref[0] = (acc_ref[...] / jnp.where(lo, l_ref[0], l_ref[1])).astype(o_ref.dtype)


def _fox_attention(qb, kb, vb, lcq, lck, past, tq, tk):
    b, t_q, _ = qb.shape
    t_k = kb.shape[1]
    nq, nk = t_q // tq, t_k // tk
    last_kv = lambda qi: (past + qi * tq + tq - 1) // tk
    kv_spec = pl.BlockSpec((1, tk, LANES), lambda i, p, qi, ki: (i, jnp.minimum(ki, last_kv(qi)), p))
    return pl.pallas_call(
        functools.partial(_attn_kernel, tq=tq, tk=tk, past=past, nk=nk),
        out_shape=jax.ShapeDtypeStruct((b, t_q, WIDTH), BF16),
        grid=(b, N_PAIRS, nq, nk),
        in_specs=[pl.BlockSpec((1, tq, LANES), lambda i, p, qi, ki: (i, qi, p)),
                  kv_spec, kv_spec,
                  pl.BlockSpec((1, 1, tq, 2), lambda i, p, qi, ki: (i, p, qi, 0)),
                  pl.BlockSpec((1, 1, 2, tk), lambda i, p, qi, ki: (i, p, 0, jnp.minimum(ki, last_kv(qi))))],
        out_specs=pl.BlockSpec((1, tq, LANES), lambda i, p, qi, ki: (i, qi, p)),
        scratch_shapes=[pltpu.VMEM((2, tq, 1), F32), pltpu.VMEM((2, tq, 1), F32), pltpu.VMEM((tq, LANES), F32)],
        compiler_params=_cparams(("arbitrary", "arbitrary", "arbitrary", "arbitrary")),
        name="fox_attention",
    )(qb, kb, vb, lcq, lck)


def _out_kernel(x_ref, ys_ref, g_ref, bonus_ref, yf_ref, mod_ref, gng_ref, gnb_ref, bd_ref, wo_ref, n2_ref,
                wg_ref, wu_ref, wd_ref, o_ref):
    bd = bd_ref[...]
    gt1 = mod_ref[0, 2:3, :]
    sh2 = mod_ref[0, 3:4, :]
    sc2 = mod_ref[0, 4:5, :]
    gt2 = mod_ref[0, 5:6, :]
    inv_hd = 1.0 / HEAD_DIM
    ys = ys_ref[0]
    d = ys - _split2_dot(ys, bd) * inv_hd
    var = _split2_dot(d * d, bd) * inv_hd
    yn = d * lax.rsqrt(var + GN_EPS) * gng_ref[...] + gnb_ref[...]
    y_rwkv = ((yn + bonus_ref[0]) * g_ref[0]).astype(BF16)
    mix = _dot(y_rwkv, wo_ref[0:WIDTH, :]) + _dot(yf_ref[0], wo_ref[WIDTH:2 * WIDTH, :])
    x1 = x_ref[0] + gt1 * mix
    h2 = x1 * lax.rsqrt(jnp.mean(x1 * x1, axis=-1, keepdims=True) + NORM_EPS) * n2_ref[...]
    hb = (h2 * (1.0 + sc2) + sh2).astype(BF16)
    act = (_silu(_dot(hb, wg_ref[...])) * _dot(hb, wu_ref[...])).astype(BF16)
    o_ref[0] = x1 + gt2 * _dot(act, wd_ref[...])


def _outproj_ffn(x, ys, g, bonus, yf, modp, wts, tm):
    b, t, _ = x.shape
    tile = lambda w: pl.BlockSpec((1, tm, w), lambda i, j: (i, j, 0))
    consts = [wts[n] for n in ("gn_g", "gn_b", "bd", "w_out", "norm2_g", "w_ffn_gate", "w_ffn_up", "w_ffn_down")]
    return pl.pallas_call(
        _out_kernel,
        out_shape=jax.ShapeDtypeStruct((b, t, D_MODEL), F32),
        grid=(b, t // tm),
        in_specs=([tile(D_MODEL), tile(WIDTH), tile(WIDTH), tile(WIDTH), tile(WIDTH),
                   pl.BlockSpec((1, 6, D_MODEL), lambda i, j: (i, 0, 0))]
                  + [_const_spec(c.shape) for c in consts]),
        out_specs=tile(D_MODEL),
        compiler_params=_cparams(("arbitrary", "arbitrary")),
        name="outproj_ffn",
    )(x, ys, g, bonus, yf, modp, *consts)


def _prep_weights(norm1_g, w_in, shift_mu, w0, w_decay_up, a0, w_aaa_up, w_gate_up, k_k, k_a, r_k, gn_g, gn_b,
                  fox_q_g, fox_k_g, fox_f_b, w_out, norm2_g, w_ffn_gate, w_ffn_up, w_ffn_down):
    row = lambda a: a.reshape(1, -1).astype(F32)
    fox0 = RWKV_COLS
    zeros_lora = jnp.zeros((DECAY_LORA, WIDTH), BF16)
    head = jnp.arange(WIDTH) // HEAD_DIM
    return {
        "norm1_g": row(norm1_g),
        "w_r": w_in[:, :RWKV_COLS].astype(BF16),
        "w_f": w_in[:, fox0:fox0 + 3 * WIDTH].astype(BF16),
        "w_fl": jnp.pad(w_in[:, fox0 + 3 * WIDTH:], ((0, 0), (0, LANES - N_HEADS))).astype(BF16),
        "shift_mu": row(shift_mu), "w0": row(w0), "a0": row(a0),
        "w_du": jnp.concatenate([w_decay_up.astype(BF16), zeros_lora], axis=0),
        "w_au": jnp.concatenate([zeros_lora, w_aaa_up.astype(BF16)], axis=0),
        "w_gu": w_gate_up.astype(BF16),
        "k_k": row(k_k), "k_a": row(k_a), "r_k": row(r_k),
        "fox_q_g": row(jnp.tile(fox_q_g, N_HEADS)), "fox_k_g": row(jnp.tile(fox_k_g, N_HEADS)),
        "fox_f_b": jnp.pad(row(fox_f_b), ((0, 0), (0, LANES - N_HEADS))),
        "bd": (head[:, None] == head[None, :]).astype(BF16),
        "gn_g": row(gn_g), "gn_b": row(gn_b),
        "w_out": w_out.astype(BF16), "norm2_g": row(norm2_g),
        "w_ffn_gate": w_ffn_gate.astype(BF16), "w_ffn_up": w_ffn_up.astype(BF16),
        "w_ffn_down": w_ffn_down.astype(BF16),
    }


def _to_pairs(state):
    b = state.shape[0]
    s = state.reshape(b, N_PAIRS, 2, HEAD_DIM, HEAD_DIM)
    return jnp.transpose(s, (0, 1, 4, 2, 3)).reshape(b, N_PAIRS, HEAD_DIM, LANES)


def _from_pairs(state):
    b = state.shape[0]
    s = state.reshape(b, N_PAIRS, HEAD_DIM, 2, HEAD_DIM)
    return jnp.transpose(s, (0, 1, 3, 4, 2)).reshape(b, N_HEADS, HEAD_DIM, HEAD_DIM)


def _round_up(n, m):
    return -(-n // m) * m


def _layer(x, modp, shift_prev, s_prev, k_past, v_past, logf_past, wts, tm, tq):
    b, t, _ = x.shape
    past = k_past.shape[1]
    (r, w, k2, v, kk, bvec, g, bonus, qb, kb, vb, kf, vf, lf_pad, new_shift) = _inproj(x, modp, shift_prev, wts, tm)

    ys, s_new = _rwkv_scan(r, w, k2, v, kk, bvec, _to_pairs(s_prev.astype(F32)))

    logf = lf_pad[:, :, :N_HEADS]
    t_all = past + t
    tk_all = _round_up(t_all, LANES)
    lf_all = jnp.concatenate([logf_past.astype(F32), logf], axis=1)
    lf_all = jnp.pad(lf_all, ((0, 0), (0, tk_all - t_all), (0, 0)))
    lc = _cumsum(jnp.transpose(lf_all, (0, 2, 1)))
    lc_pairs = lc.reshape(b, N_PAIRS, 2, tk_all)
    lcq = jnp.transpose(lc_pairs[:, :, :, past:past + t], (0, 1, 3, 2))
    kv_pad = ((0, 0), (0, tk_all - t_all), (0, 0))
    k_all = jnp.pad(jnp.concatenate([k_past.reshape(b, past, WIDTH).astype(BF16), kb], axis=1), kv_pad)
    v_all = jnp.pad(jnp.concatenate([v_past.reshape(b, past, WIDTH).astype(BF16), vb], axis=1), kv_pad)
    tk = tq if tk_all % tq == 0 and t > tq else tk_all
    yf = _fox_attention(qb, k_all, v_all, lcq, lc_pairs, past, min(tq, t), tk)

    y = _outproj_ffn(x, ys, g, bonus, yf, modp, wts, tm)
    return (y, _from_pairs(s_new), new_shift, kf.reshape(b, t, N_HEADS, HEAD_DIM),
            vf.reshape(b, t, N_HEADS, HEAD_DIM), logf)


def kernel(x_prompt, x_sample, cache_fox_k, cache_fox_v, cache_fox_logf, state_rwkv, state_rwkv_shift, c_prompt, c_sample, norm1_g, w_ada, b_ada, w_in, shift_mu, w0, w_decay_up, a0, w_aaa_up, w_gate_up, k_k, k_a, r_k, gn_g, gn_b, fox_q_g, fox_k_g, fox_f_b, w_out, norm2_g, w_ffn_gate, w_ffn_up, w_ffn_down):
    depth = w_in.shape[0]
    bp, bs = x_prompt.shape[0], x_sample.shape[0]
    zero_shift = jnp.zeros((bp, 1, RWKV_COLS), F32)
    zero_state = jnp.zeros((bp, N_HEADS, HEAD_DIM, HEAD_DIM), F32)
    zero_kv = jnp.zeros((bp, 0, N_HEADS, HEAD_DIM), F32)
    zero_logf = jnp.zeros((bp, 0, N_HEADS), F32)
    hp, hs = x_prompt, x_sample
    outs_p, outs_s = [], []
    for l in range(depth):
        wts = _prep_weights(norm1_g[l], w_in[l], shift_mu[l], w0[l], w_decay_up[l], a0[l], w_aaa_up[l],
                            w_gate_up[l], k_k[l], k_a[l], r_k[l], gn_g[l], gn_b[l], fox_q_g[l], fox_k_g[l],
                            fox_f_b[l], w_out[l], norm2_g[l], w_ffn_gate[l], w_ffn_up[l], w_ffn_down[l])
        rows = _round_up(bp + bs, 8)
        c_all = jnp.pad(jnp.concatenate([c_prompt, c_sample], axis=0), ((0, rows - bp - bs), (0, 0)))
        mod = _adaln(c_all, w_ada[l].astype(BF16), b_ada[l].reshape(1, -1)).reshape(rows, 6, D_MODEL)
        res_p = _layer(hp, mod[:bp], zero_shift, zero_state, zero_kv, zero_kv, zero_logf, wts, tm=256, tq=512)
        res_s = _layer(hs, mod[bp:bp + bs], state_rwkv_shift[l], state_rwkv[l], cache_fox_k[l], cache_fox_v[l],
                       cache_fox_logf[l], wts, tm=64, tq=512)
        hp, hs = res_p[0], res_s[0]
        outs_p.append(res_p[1:])
        outs_s.append(res_s[1:])
    stack = lambda outs, i: jnp.stack([o[i] for o in outs])
    return (hp, hs,
            stack(outs_p, 0), stack(outs_p, 1), stack(outs_p, 2), stack(outs_p, 3), stack(outs_p, 4),
            stack(outs_s, 0), stack(outs_s, 1), stack(outs_s, 2), stack(outs_s, 3), stack(outs_s, 4))
```

```python
import functools

import jax
import jax.numpy as jnp
from jax import lax
from jax.experimental import pallas as pl
from jax.experimental.pallas import tpu as pltpu

F32 = jnp.float32
BF16 = jnp.bfloat16

D_MODEL = 1024
HEAD_DIM = 64
N_HEADS = 8
WIDTH = N_HEADS * HEAD_DIM
N_PAIRS = N_HEADS // 2
DECAY_LORA = 64
AAA_LORA = 64
GATE_LORA = 128
RWKV_COLS = 3 * WIDTH + DECAY_LORA + AAA_LORA + GATE_LORA
LORA_OFF = 3 * WIDTH
D_FF = 2816
NORM_EPS = 1e-6
GN_EPS = 64e-5
LANES = 128
SUBLANES = 8
SCAN_CHUNK = 512
VMEM_LIMIT = 56 * 1024 * 1024


def _cparams(sem):
    return pltpu.CompilerParams(dimension_semantics=sem, vmem_limit_bytes=VMEM_LIMIT)


def _const_spec(shape):
    nd = len(shape)
    return pl.BlockSpec(shape, lambda *_: (0,) * nd, pipeline_mode=pl.Buffered(1))


def _softplus(x):
    return jnp.maximum(x, 0.0) + jnp.log1p(jnp.exp(-jnp.abs(x)))


def _silu(x):
    return x * jax.nn.sigmoid(x)


def _dot(a, b):
    return jnp.dot(a, b, preferred_element_type=F32)


def _split2_dot(x, m):
    hi = x.astype(BF16)
    lo = (x - hi.astype(F32)).astype(BF16)
    return _dot(hi, m) + _dot(lo, m)


def _split3_dot_rhs(m, x):
    hi = x.astype(BF16)
    r1 = x - hi.astype(F32)
    mid = r1.astype(BF16)
    lo = (r1 - mid.astype(F32)).astype(BF16)
    return _dot(m, hi) + _dot(m, mid) + _dot(m, lo)


def _split3_dot_lhs(x, m):
    hi = x.astype(BF16)
    r1 = x - hi.astype(F32)
    mid = r1.astype(BF16)
    lo = (r1 - mid.astype(F32)).astype(BF16)
    return _dot(hi, m) + _dot(mid, m) + _dot(lo, m)


def _mod_kernel(c_ref, w_ref, b_ref, o_ref):
    o_ref[...] = _dot(_silu(c_ref[...]).astype(BF16), w_ref[...]) + b_ref[...]


def _adaln(c_all, w_ada_bf, b_ada):
    rows = c_all.shape[0]
    n = w_ada_bf.shape[1]
    bn = D_MODEL
    return pl.pallas_call(
        _mod_kernel,
        out_shape=jax.ShapeDtypeStruct((rows, n), F32),
        grid=(n // bn,),
        in_specs=[pl.BlockSpec((rows, D_MODEL), lambda j: (0, 0)),
                  pl.BlockSpec((D_MODEL, bn), lambda j: (0, j)),
                  pl.BlockSpec((1, bn), lambda j: (0, j))],
        out_specs=pl.BlockSpec((rows, bn), lambda j: (0, j)),
        compiler_params=_cparams(("arbitrary",)),
        name="adaln_mod",
    )(c_all, w_ada_bf, b_ada)


def _inproj_kernel(x_ref, mod_ref, g1_ref, wr_ref, wf_ref, wfl_ref, sprev_ref, mu_ref, w0_ref, a0_ref,
                   wdu_ref, wau_ref, wgu_ref, kk_ref, ka_ref, rk_ref, qg_ref, kg_ref, fb_ref, bd_ref,
                   r_o, w_o, k_o, v_o, kk_o, b_o, g_o, bonus_o, qb_o, kb_o, vb_o, kf_o, vf_o, lf_o, sh_o,
                   carry_ref, *, tm):
    t = pl.program_id(1)
    bd = bd_ref[...]
    x = x_ref[0]
    sh1 = mod_ref[0, 0:1, :]
    sc1 = mod_ref[0, 1:2, :]
    h = x * lax.rsqrt(jnp.mean(x * x, axis=-1, keepdims=True) + NORM_EPS) * g1_ref[...]
    hb = (h * (1.0 + sc1) + sh1).astype(BF16)
    pr = _dot(hb, wr_ref[...])
    pf = _dot(hb, wf_ref[...])
    pfl = _dot(hb, wfl_ref[...])

    @pl.when(t == 0)
    def _():
        carry_ref[...] = sprev_ref[0]

    prev = pltpu.roll(pr, 1, axis=0)
    first_row = lax.broadcasted_iota(jnp.int32, pr.shape, 0) == 0
    prev = jnp.where(first_row, carry_ref[...], prev)
    last = pr[tm - 1:tm, :]
    carry_ref[...] = last
    sh_o[0] = last
    z = pr + (prev - pr) * mu_ref[...]

    r = z[:, 0:WIDTH]
    k = z[:, WIDTH:2 * WIDTH]
    v = z[:, 2 * WIDTH:3 * WIDTH]
    dwa = z[:, LORA_OFF:LORA_OFF + LANES]
    dg = z[:, LORA_OFF + LANES:RWKV_COLS]
    w_log = -_softplus(-(w0_ref[...] + _dot(jnp.tanh(dwa).astype(BF16), wdu_ref[...]))) - 0.5
    decay = jnp.exp(-jnp.exp(w_log))
    a = jax.nn.sigmoid(a0_ref[...] + _dot(dwa.astype(BF16), wau_ref[...]))
    g = _dot(jax.nn.sigmoid(dg).astype(BF16), wgu_ref[...])
    kk = k * kk_ref[...]
    kk = kk / jnp.maximum(jnp.sqrt(_split2_dot(kk * kk, bd)), 1e-12)
    k2 = k * (1.0 + (a - 1.0) * ka_ref[...])
    bonus = _split2_dot(r * k2 * rk_ref[...], bd) * v

    r_o[0] = r
    w_o[0] = decay
    k_o[0] = k2
    v_o[0] = v
    kk_o[0] = kk
    b_o[0] = kk * a
    g_o[0] = g
    bonus_o[0] = bonus

    q = pf[:, 0:WIDTH]
    kf = pf[:, WIDTH:2 * WIDTH]
    vf = pf[:, 2 * WIDTH:3 * WIDTH]
    inv_hd = 1.0 / HEAD_DIM
    qn = q * lax.rsqrt(_split2_dot(q * q, bd) * inv_hd + NORM_EPS) * qg_ref[...]
    kn = kf * lax.rsqrt(_split2_dot(kf * kf, bd) * inv_hd + NORM_EPS) * kg_ref[...]
    qb_o[0] = (qn * (HEAD_DIM ** -0.5)).astype(BF16)
    kb_o[0] = kn.astype(BF16)
    vb_o[0] = vf.astype(BF16)
    kf_o[0] = kn
    vf_o[0] = vf
    lf_o[0] = -_softplus(-(pfl + fb_ref[...]))


def _inproj(x, modp, shift_prev, wts, tm):
    b, t, _ = x.shape
    nt = t // tm
    tok = lambda w, dt: jax.ShapeDtypeStruct((b, t, w), dt)
    tile = lambda w: pl.BlockSpec((1, tm, w), lambda i, j: (i, j, 0))
    consts = [wts[n] for n in ("norm1_g", "w_r", "w_f", "w_fl")]
    consts2 = [wts[n] for n in ("shift_mu", "w0", "a0", "w_du", "w_au", "w_gu", "k_k", "k_a", "r_k",
                                "fox_q_g", "fox_k_g", "fox_f_b", "bd")]
    in_specs = ([tile(D_MODEL), pl.BlockSpec((1, 6, D_MODEL), lambda i, j: (i, 0, 0))]
                + [_const_spec(c.shape) for c in consts]
                + [pl.BlockSpec((1, 1, RWKV_COLS), lambda i, j: (i, 0, 0))]
                + [_const_spec(c.shape) for c in consts2])
    out_shape = ([tok(WIDTH, F32)] * 8 + [tok(WIDTH, BF16)] * 3 + [tok(WIDTH, F32)] * 2
                 + [tok(LANES, F32), jax.ShapeDtypeStruct((b, 1, RWKV_COLS), F32)])
    out_specs = ([tile(WIDTH)] * 13 + [tile(LANES), pl.BlockSpec((1, 1, RWKV_COLS), lambda i, j: (i, 0, 0))])
    return pl.pallas_call(
        functools.partial(_inproj_kernel, tm=tm),
        out_shape=out_shape,
        grid=(b, nt),
        in_specs=in_specs,
        out_specs=out_specs,
        scratch_shapes=[pltpu.VMEM((1, RWKV_COLS), F32)],
        compiler_params=_cparams(("arbitrary", "arbitrary")),
        name="inproj",
    )(x, modp, *consts, shift_prev, *consts2)


def _exact_pieces(x):
    hi = x.astype(BF16).astype(F32)
    r1 = x - hi
    mid = r1.astype(BF16).astype(F32)
    return hi, mid, r1 - mid


def _scan_kernel(r_ref, w_ref, k_ref, v_ref, kk_ref, b_ref, s0_ref, sel_ref, e_ref, y_ref, s_ref,
                 tile_ref, gam_ref, *, n_steps):
    t = pl.program_id(1)

    @pl.when(t == 0)
    def _():
        s_ref[...] = s0_ref[...]

    sel = sel_ref[...]
    e_all = e_ref[...]
    row_id = lax.broadcasted_iota(jnp.int32, (SUBLANES, LANES), 0)
    lo8 = lax.broadcasted_iota(jnp.int32, (SUBLANES, LANES), 1) < HEAD_DIM
    q_pad = jnp.zeros((LANES - 6 * SUBLANES, LANES), F32)

    def transposed_halves(x8):
        blocks = []
        for h in range(2):
            blocks.extend(_exact_pieces(jnp.where(lo8 if h == 0 else jnp.logical_not(lo8), x8, 0.0)))
        q = jnp.concatenate(blocks + [q_pad], axis=0).astype(BF16)
        return lax.dot_general(sel, q, (((1,), (1,)), ((), ())), preferred_element_type=F32).astype(BF16)

    def shift_rows(x, sh, fill):
        return jnp.where(row_id >= sh, pltpu.roll(x, sh, axis=0), fill)

    n_groups = n_steps // SUBLANES
    last_lanes = (SUBLANES - 1) * LANES

    def make_tiles(group, slot, p):
        base = pl.multiple_of(group * SUBLANES, SUBLANES)
        rows8 = lambda ref: ref[0, pl.ds(base, SUBLANES), pl.ds(p * LANES, LANES)]
        g_inc = rows8(w_ref)
        for sh in (1, 2, 4):
            g_inc = g_inc * shift_rows(g_inc, sh, 1.0)
        g_exc = shift_rows(g_inc, 1, 1.0)
        inv = 1.0 / g_inc
        scaled = (rows8(kk_ref) * g_exc, rows8(b_ref) * inv, rows8(k_ref) * inv, rows8(r_ref) * g_inc)
        lhs = jnp.concatenate([transposed_halves(x) for x in scaled], axis=0)
        tile_ref[slot, p] = _dot(lhs, e_all)
        gam_ref[slot, p] = _dot(transposed_halves(g_inc), e_all[:, last_lanes:last_lanes + LANES])

    def run_tokens(group, slot, states, next_group):
        base = pl.multiple_of(group * SUBLANES, SUBLANES)
        states = list(states)
        v8 = [v_ref[0, pl.ds(base, SUBLANES), pl.ds(p * LANES, LANES)] for p in range(N_PAIRS)]
        ys = [[] for _ in range(N_PAIRS)]
        for u in range(SUBLANES):
            if u % 2 == 0:
                make_tiles(next_group, 1 - slot, u // 2)
            for p in range(N_PAIRS):
                col = lambda vec: tile_ref[slot, p, vec * HEAD_DIM:(vec + 1) * HEAD_DIM, u * LANES:(u + 1) * LANES]
                s = states[p]
                sa = jnp.sum(s * col(0), axis=0, keepdims=True)
                s = s + col(2) * v8[p][u:u + 1, :] - col(1) * sa
                ys[p].append(jnp.sum(s * col(3), axis=0, keepdims=True))
                states[p] = s
        for p in range(N_PAIRS):
            states[p] = states[p] * gam_ref[slot, p]
            y_ref[0, pl.ds(base, SUBLANES), pl.ds(p * LANES, LANES)] = jnp.concatenate(ys[p], axis=0)
        return tuple(states)

    def body(jj, states):
        g0 = 2 * jj
        states = run_tokens(g0, 0, states, g0 + 1)
        return run_tokens(g0 + 1, 1, states, jnp.minimum(g0 + 2, n_groups - 1))

    for p in range(N_PAIRS):
        make_tiles(0, 0, p)
    init = tuple(s_ref[0, p] for p in range(N_PAIRS))
    final = lax.fori_loop(0, n_groups // 2, body, init)
    for p in range(N_PAIRS):
        s_ref[0, p] = final[p]


def _rwkv_scan(r, w, k, v, kk, bvec, s0_pairs):
    b, t, _ = r.shape
    chunk = min(SCAN_CHUNK, t)
    assert t % chunk == 0 and chunk % SUBLANES == 0
    nt = t // chunk
    lane = jnp.arange(LANES)
    sel = (lane[None, :] % HEAD_DIM == jnp.arange(HEAD_DIM)[:, None]).astype(BF16)
    q_row = jnp.arange(LANES)
    q_half, q_tok, q_used = q_row // (3 * SUBLANES), q_row % SUBLANES, q_row < 6 * SUBLANES
    col = jnp.arange(SUBLANES * LANES)
    e_all = (q_used[:, None] & (q_tok[:, None] == col[None, :] // LANES)
             & (q_half[:, None] == (col[None, :] % LANES) // HEAD_DIM)).astype(BF16)
    tile = pl.BlockSpec((1, chunk, WIDTH), lambda i, j: (i, j, 0))
    st = pl.BlockSpec((1, N_PAIRS, HEAD_DIM, LANES), lambda i, j: (i, 0, 0, 0))
    return pl.pallas_call(
        functools.partial(_scan_kernel, n_steps=chunk),
        out_shape=[jax.ShapeDtypeStruct((b, t, WIDTH), F32),
                   jax.ShapeDtypeStruct((b, N_PAIRS, HEAD_DIM, LANES), F32)],
        grid=(b, nt),
        in_specs=[tile] * 6 + [st, _const_spec(sel.shape), _const_spec(e_all.shape)],
        out_specs=[tile, st],
        scratch_shapes=[pltpu.VMEM((2, N_PAIRS, 4 * HEAD_DIM, SUBLANES * LANES), F32),
                        pltpu.VMEM((2, N_PAIRS, HEAD_DIM, LANES), F32)],
        compiler_params=_cparams(("arbitrary", "arbitrary")),
        name="rwkv_scan",
    )(r, w, k, v, kk, bvec, s0_pairs, sel, e_all)


def _cumsum_kernel(x_ref, u_ref, l_ref, o_ref):
    x = x_ref[0]
    local = _split3_dot_lhs(x, u_ref[...])
    tot = jnp.broadcast_to(local[:, LANES - 1:LANES], local.shape)
    o_ref[0] = local + _split3_dot_rhs(l_ref[...], tot)


def _cumsum(logf_t):
    b, h, t = logf_t.shape
    nb = t // LANES
    rows = h * nb
    x = logf_t.reshape(b, rows, LANES)
    col = jnp.arange(LANES)
    u = (col[:, None] <= col[None, :]).astype(BF16)
    rid = jnp.arange(rows)
    l = ((rid[:, None] // nb == rid[None, :] // nb) & (rid[None, :] < rid[:, None])).astype(BF16)
    out = pl.pallas_call(
        _cumsum_kernel,
        out_shape=jax.ShapeDtypeStruct((b, rows, LANES), F32),
        grid=(b,),
        in_specs=[pl.BlockSpec((1, rows, LANES), lambda i: (i, 0, 0)),
                  _const_spec(u.shape), _const_spec(l.shape)],
        out_specs=pl.BlockSpec((1, rows, LANES), lambda i: (i, 0, 0)),
        compiler_params=_cparams(("arbitrary",)),
        name="logf_cumsum",
    )(x, u, l)
    return out.reshape(b, h, t)


N_BIAS = 6


def _attn_kernel(q_ref, k_ref, vt_ref, o_ref, m_ref, l_ref, acc_ref, *, tq, tk, past):
    qi = pl.program_id(2)
    q_first = past + qi * tq
    n_full = q_first // tk

    m_ref[...] = jnp.full_like(m_ref, -jnp.inf)
    l_ref[...] = jnp.zeros_like(l_ref)
    acc_ref[...] = jnp.zeros_like(acc_ref)

    def kv_block(j, masked):
        k_start = pl.multiple_of(j * tk, tk)
        for hb in range(2):
            s = lax.dot_general(k_ref[0, hb, pl.ds(k_start, tk), :], q_ref[0, hb], (((1,), (1,)), ((), ())),
                                preferred_element_type=F32)
            if masked:
                k_pos = k_start + lax.broadcasted_iota(jnp.int32, (tk, tq), 0)
                q_pos = q_first + lax.broadcasted_iota(jnp.int32, (tk, tq), 1)
                s = jnp.where(k_pos <= q_pos, s, -jnp.inf)
            m_prev = m_ref[hb]
            m_new = jnp.maximum(m_prev, jnp.max(s, axis=0, keepdims=True))
            alpha = jnp.exp(m_prev - m_new)
            p = jnp.exp(s - m_new)
            l_ref[hb] = alpha * l_ref[hb] + jnp.sum(p, axis=0, keepdims=True)
            m_ref[hb] = m_new
            acc_ref[hb] = alpha * acc_ref[hb] + _dot(vt_ref[0, hb, :, pl.ds(k_start, tk)], p.astype(BF16))

    def full_block(j, carry):
        kv_block(j, False)
        return carry

    lax.fori_loop(0, n_full, full_block, 0)
    kv_block(n_full, True)
    out_t = jnp.concatenate([acc_ref[0] / l_ref[0], acc_ref[1] / l_ref[1]], axis=0)
    o_ref[0] = out_t.T.astype(o_ref.dtype)


def _bias_lanes(lc, sign_first):
    hi = lc.astype(BF16)
    r1 = lc - hi.astype(F32)
    mid = r1.astype(BF16)
    lo = (r1 - mid.astype(F32)).astype(BF16)
    ones = jnp.ones_like(hi)
    cols = [hi, mid, lo, ones, ones, ones] if sign_first else [ones, ones, ones, -hi, -mid, -lo]
    return jnp.stack(cols, axis=-1)


def _head_major_aug(x, bias):
    b, t, _ = x.shape
    xh = jnp.transpose(x.reshape(b, t, N_HEADS, HEAD_DIM), (0, 2, 1, 3))
    pad = jnp.zeros((b, N_HEADS, t, LANES - HEAD_DIM - N_BIAS), BF16)
    return jnp.concatenate([xh, bias, pad], axis=-1)


def _fox_attention(qb, kb, vb, lc, past, tq, tk):
    b, t_q, _ = qb.shape
    t_k = kb.shape[1]
    assert t_q % tq == 0 and t_k % tk == 0 and tk % tq == 0 and past % tq == 0
    q_aug = _head_major_aug(qb, _bias_lanes(lc[:, :, past:past + t_q], True))
    k_aug = _head_major_aug(kb, _bias_lanes(lc, False))
    v_t = jnp.transpose(vb.reshape(b, t_k, N_HEADS, HEAD_DIM), (0, 2, 3, 1))
    return pl.pallas_call(
        functools.partial(_attn_kernel, tq=tq, tk=tk, past=past),
        out_shape=jax.ShapeDtypeStruct((b, t_q, WIDTH), BF16),
        grid=(b, N_PAIRS, t_q // tq),
        in_specs=[pl.BlockSpec((1, 2, tq, LANES), lambda i, p, qi: (i, p, qi, 0)),
                  pl.BlockSpec((1, 2, t_k, LANES), lambda i, p, qi: (i, p, 0, 0)),
                  pl.BlockSpec((1, 2, HEAD_DIM, t_k), lambda i, p, qi: (i, p, 0, 0))],
        out_specs=pl.BlockSpec((1, tq, LANES), lambda i, p, qi: (i, qi, p)),
        scratch_shapes=[pltpu.VMEM((2, 1, tq), F32), pltpu.VMEM((2, 1, tq), F32),
                        pltpu.VMEM((2, HEAD_DIM, tq), F32)],
        compiler_params=_cparams(("arbitrary", "arbitrary", "arbitrary")),
        name="fox_attention",
    )(q_aug, k_aug, v_t)


def _out_kernel(x_ref, ys_ref, g_ref, bonus_ref, yf_ref, mod_ref, gng_ref, gnb_ref, bd_ref, wo_ref, n2_ref,
                wg_ref, wu_ref, wd_ref, o_ref):
    bd = bd_ref[...]
    gt1 = mod_ref[0, 2:3, :]
    sh2 = mod_ref[0, 3:4, :]
    sc2 = mod_ref[0, 4:5, :]
    gt2 = mod_ref[0, 5:6, :]
    inv_hd = 1.0 / HEAD_DIM
    ys = ys_ref[0]
    d = ys - _split2_dot(ys, bd) * inv_hd
    var = _split2_dot(d * d, bd) * inv_hd
    yn = d * lax.rsqrt(var + GN_EPS) * gng_ref[...] + gnb_ref[...]
    y_rwkv = ((yn + bonus_ref[0]) * g_ref[0]).astype(BF16)
    mix = _dot(y_rwkv, wo_ref[0:WIDTH, :]) + _dot(yf_ref[0], wo_ref[WIDTH:2 * WIDTH, :])
    x1 = x_ref[0] + gt1 * mix
    h2 = x1 * lax.rsqrt(jnp.mean(x1 * x1, axis=-1, keepdims=True) + NORM_EPS) * n2_ref[...]
    hb = (h2 * (1.0 + sc2) + sh2).astype(BF16)
    act = (_silu(_dot(hb, wg_ref[...])) * _dot(hb, wu_ref[...])).astype(BF16)
    o_ref[0] = x1 + gt2 * _dot(act, wd_ref[...])


def _outproj_ffn(x, ys, g, bonus, yf, modp, wts, tm):
    b, t, _ = x.shape
    tile = lambda w: pl.BlockSpec((1, tm, w), lambda i, j: (i, j, 0))
    consts = [wts[n] for n in ("gn_g", "gn_b", "bd", "w_out", "norm2_g", "w_ffn_gate", "w_ffn_up", "w_ffn_down")]
    return pl.pallas_call(
        _out_kernel,
        out_shape=jax.ShapeDtypeStruct((b, t, D_MODEL), F32),
        grid=(b, t // tm),
        in_specs=([tile(D_MODEL), tile(WIDTH), tile(WIDTH), tile(WIDTH), tile(WIDTH),
                   pl.BlockSpec((1, 6, D_MODEL), lambda i, j: (i, 0, 0))]
                  + [_const_spec(c.shape) for c in consts]),
        out_specs=tile(D_MODEL),
        compiler_params=_cparams(("arbitrary", "arbitrary")),
        name="outproj_ffn",
    )(x, ys, g, bonus, yf, modp, *consts)


def _prep_weights(norm1_g, w_in, shift_mu, w0, w_decay_up, a0, w_aaa_up, w_gate_up, k_k, k_a, r_k, gn_g, gn_b,
                  fox_q_g, fox_k_g, fox_f_b, w_out, norm2_g, w_ffn_gate, w_ffn_up, w_ffn_down):
    row = lambda a: a.reshape(1, -1).astype(F32)
    fox0 = RWKV_COLS
    zeros_lora = jnp.zeros((DECAY_LORA, WIDTH), BF16)
    head = jnp.arange(WIDTH) // HEAD_DIM
    return {
        "norm1_g": row(norm1_g),
        "w_r": w_in[:, :RWKV_COLS].astype(BF16),
        "w_f": w_in[:, fox0:fox0 + 3 * WIDTH].astype(BF16),
        "w_fl": jnp.pad(w_in[:, fox0 + 3 * WIDTH:], ((0, 0), (0, LANES - N_HEADS))).astype(BF16),
        "shift_mu": row(shift_mu), "w0": row(w0), "a0": row(a0),
        "w_du": jnp.concatenate([w_decay_up.astype(BF16), zeros_lora], axis=0),
        "w_au": jnp.concatenate([zeros_lora, w_aaa_up.astype(BF16)], axis=0),
        "w_gu": w_gate_up.astype(BF16),
        "k_k": row(k_k), "k_a": row(k_a), "r_k": row(r_k),
        "fox_q_g": row(jnp.tile(fox_q_g, N_HEADS)), "fox_k_g": row(jnp.tile(fox_k_g, N_HEADS)),
        "fox_f_b": jnp.pad(row(fox_f_b), ((0, 0), (0, LANES - N_HEADS))),
        "bd": (head[:, None] == head[None, :]).astype(BF16),
        "gn_g": row(gn_g), "gn_b": row(gn_b),
        "w_out": w_out.astype(BF16), "norm2_g": row(norm2_g),
        "w_ffn_gate": w_ffn_gate.astype(BF16), "w_ffn_up": w_ffn_up.astype(BF16),
        "w_ffn_down": w_ffn_down.astype(BF16),
    }


def _to_pairs(state):
    b = state.shape[0]
    s = state.reshape(b, N_PAIRS, 2, HEAD_DIM, HEAD_DIM)
    return jnp.transpose(s, (0, 1, 4, 2, 3)).reshape(b, N_PAIRS, HEAD_DIM, LANES)


def _from_pairs(state):
    b = state.shape[0]
    s = state.reshape(b, N_PAIRS, HEAD_DIM, 2, HEAD_DIM)
    return jnp.transpose(s, (0, 1, 3, 4, 2)).reshape(b, N_HEADS, HEAD_DIM, HEAD_DIM)


def _round_up(n, m):
    return -(-n // m) * m


def _layer(x, modp, shift_prev, s_prev, k_past, v_past, logf_past, wts, tm, tq):
    b, t, _ = x.shape
    past = k_past.shape[1]
    (r, w, k2, v, kk, bvec, g, bonus, qb, kb, vb, kf, vf, lf_pad, new_shift) = _inproj(x, modp, shift_prev, wts, tm)

    ys, s_new = _rwkv_scan(r, w, k2, v, kk, bvec, _to_pairs(s_prev.astype(F32)))

    logf = lf_pad[:, :, :N_HEADS]
    t_all = past + t
    tk_all = _round_up(t_all, LANES)
    lf_all = jnp.concatenate([logf_past.astype(F32), logf], axis=1)
    lf_all = jnp.pad(lf_all, ((0, 0), (0, tk_all - t_all), (0, 0)))
    lc = _cumsum(jnp.transpose(lf_all, (0, 2, 1)))
    kv_pad = ((0, 0), (0, tk_all - t_all), (0, 0))
    k_all = jnp.pad(jnp.concatenate([k_past.reshape(b, past, WIDTH).astype(BF16), kb], axis=1), kv_pad)
    v_all = jnp.pad(jnp.concatenate([v_past.reshape(b, past, WIDTH).astype(BF16), vb], axis=1), kv_pad)
    tq = min(tq, t)
    yf = _fox_attention(qb, k_all, v_all, lc, past, tq, max(tq, LANES))

    y = _outproj_ffn(x, ys, g, bonus, yf, modp, wts, tm)
    return (y, _from_pairs(s_new), new_shift, kf.reshape(b, t, N_HEADS, HEAD_DIM),
            vf.reshape(b, t, N_HEADS, HEAD_DIM), logf)


def kernel(x_prompt, x_sample, cache_fox_k, cache_fox_v, cache_fox_logf, state_rwkv, state_rwkv_shift, c_prompt, c_sample, norm1_g, w_ada, b_ada, w_in, shift_mu, w0, w_decay_up, a0, w_aaa_up, w_gate_up, k_k, k_a, r_k, gn_g, gn_b, fox_q_g, fox_k_g, fox_f_b, w_out, norm2_g, w_ffn_gate, w_ffn_up, w_ffn_down):
    depth = w_in.shape[0]
    bp, bs = x_prompt.shape[0], x_sample.shape[0]
    zero_shift = jnp.zeros((bp, 1, RWKV_COLS), F32)
    zero_state = jnp.zeros((bp, N_HEADS, HEAD_DIM, HEAD_DIM), F32)
    zero_kv = jnp.zeros((bp, 0, N_HEADS, HEAD_DIM), F32)
    zero_logf = jnp.zeros((bp, 0, N_HEADS), F32)
    hp, hs = x_prompt, x_sample
    outs_p, outs_s = [], []
    for l in range(depth):
        wts = _prep_weights(norm1_g[l], w_in[l], shift_mu[l], w0[l], w_decay_up[l], a0[l], w_aaa_up[l],
                            w_gate_up[l], k_k[l], k_a[l], r_k[l], gn_g[l], gn_b[l], fox_q_g[l], fox_k_g[l],
                            fox_f_b[l], w_out[l], norm2_g[l], w_ffn_gate[l], w_ffn_up[l], w_ffn_down[l])
        rows = _round_up(bp + bs, 8)
        c_all = jnp.pad(jnp.concatenate([c_prompt, c_sample], axis=0), ((0, rows - bp - bs), (0, 0)))
        mod = _adaln(c_all, w_ada[l].astype(BF16), b_ada[l].reshape(1, -1)).reshape(rows, 6, D_MODEL)
        res_p = _layer(hp, mod[:bp], zero_shift, zero_state, zero_kv, zero_kv, zero_logf, wts, tm=256, tq=512)
        res_s = _layer(hs, mod[bp:bp + bs], state_rwkv_shift[l], state_rwkv[l], cache_fox_k[l], cache_fox_v[l],
                       cache_fox_logf[l], wts, tm=64, tq=512)
        hp, hs = res_p[0], res_s[0]
        outs_p.append(res_p[1:])
        outs_s.append(res_s[1:])
    stack = lambda outs, i: jnp.stack([o[i] for o in outs])
    return (hp, hs,
            stack(outs_p, 0), stack(outs_p, 1), stack(outs_p, 2), stack(outs_p, 3), stack(outs_p, 4),
            stack(outs_s, 0), stack(outs_s, 1), stack(outs_s, 2), stack(outs_s, 3), stack(outs_s, 4))
```

```python
import functools

import jax
import jax.numpy as jnp
from jax import lax
from jax.experimental import pallas as pl
from jax.experimental.pallas import tpu as pltpu

F32 = jnp.float32
BF16 = jnp.bfloat16

D_MODEL = 1024
HEAD_DIM = 64
N_HEADS = 8
WIDTH = N_HEADS * HEAD_DIM
N_PAIRS = N_HEADS // 2
DECAY_LORA = 64
AAA_LORA = 64
GATE_LORA = 128
RWKV_COLS = 3 * WIDTH + DECAY_LORA + AAA_LORA + GATE_LORA
LORA_OFF = 3 * WIDTH
D_FF = 2816
NORM_EPS = 1e-6
GN_EPS = 64e-5
LANES = 128
SUBLANES = 8
SCAN_CHUNK = 512
VMEM_LIMIT = 56 * 1024 * 1024


def _cparams(sem):
    return pltpu.CompilerParams(dimension_semantics=sem, vmem_limit_bytes=VMEM_LIMIT)


def _const_spec(shape):
    nd = len(shape)
    return pl.BlockSpec(shape, lambda *_: (0,) * nd, pipeline_mode=pl.Buffered(1))


def _softplus(x):
    return jnp.maximum(x, 0.0) + jnp.log1p(jnp.exp(-jnp.abs(x)))


def _silu(x):
    return x * jax.nn.sigmoid(x)


def _dot(a, b):
    return jnp.dot(a, b, preferred_element_type=F32)


def _split2_dot(x, m):
    hi = x.astype(BF16)
    lo = (x - hi.astype(F32)).astype(BF16)
    return _dot(hi, m) + _dot(lo, m)


def _split3_dot_rhs(m, x):
    hi = x.astype(BF16)
    r1 = x - hi.astype(F32)
    mid = r1.astype(BF16)
    lo = (r1 - mid.astype(F32)).astype(BF16)
    return _dot(m, hi) + _dot(m, mid) + _dot(m, lo)


def _split3_dot_lhs(x, m):
    hi = x.astype(BF16)
    r1 = x - hi.astype(F32)
    mid = r1.astype(BF16)
    lo = (r1 - mid.astype(F32)).astype(BF16)
    return _dot(hi, m) + _dot(mid, m) + _dot(lo, m)


def _mod_kernel(c_ref, w_ref, b_ref, o_ref):
    o_ref[...] = _dot(_silu(c_ref[...]).astype(BF16), w_ref[...]) + b_ref[...]


def _adaln(c_all, w_ada_bf, b_ada):
    rows = c_all.shape[0]
    n = w_ada_bf.shape[1]
    bn = D_MODEL
    return pl.pallas_call(
        _mod_kernel,
        out_shape=jax.ShapeDtypeStruct((rows, n), F32),
        grid=(n // bn,),
        in_specs=[pl.BlockSpec((rows, D_MODEL), lambda j: (0, 0)),
                  pl.BlockSpec((D_MODEL, bn), lambda j: (0, j)),
                  pl.BlockSpec((1, bn), lambda j: (0, j))],
        out_specs=pl.BlockSpec((rows, bn), lambda j: (0, j)),
        compiler_params=_cparams(("arbitrary",)),
        name="adaln_mod",
    )(c_all, w_ada_bf, b_ada)


def _inproj_kernel(x_ref, mod_ref, g1_ref, wr_ref, wf_ref, wfl_ref, sprev_ref, lcin_ref, mu_ref, w0_ref, a0_ref,
                   wdu_ref, wau_ref, wgu_ref, kk_ref, ka_ref, rk_ref, qg_ref, kg_ref, fb_ref, bd_ref,
                   tri_ref, pq_ref, pk_ref, oneq_ref, onek_ref,
                   r_o, w_o, k_o, v_o, kk_o, b_o, g_o, bonus_o, qa_o, ka_o, vt_o, kf_o, vf_o, lf_o, sh_o,
                   carry_ref, lc_ref, *, tm):
    t = pl.program_id(1)
    bd = bd_ref[...]
    x = x_ref[0]
    sh1 = mod_ref[0, 0:1, :]
    sc1 = mod_ref[0, 1:2, :]
    h = x * lax.rsqrt(jnp.mean(x * x, axis=-1, keepdims=True) + NORM_EPS) * g1_ref[...]
    hb = (h * (1.0 + sc1) + sh1).astype(BF16)
    pr = _dot(hb, wr_ref[...])
    pf = _dot(hb, wf_ref[...])
    pfl = _dot(hb, wfl_ref[...])

    @pl.when(t == 0)
    def _():
        carry_ref[...] = sprev_ref[0]

    prev = pltpu.roll(pr, 1, axis=0)
    first_row = lax.broadcasted_iota(jnp.int32, pr.shape, 0) == 0
    prev = jnp.where(first_row, carry_ref[...], prev)
    last = pr[tm - 1:tm, :]
    carry_ref[...] = last
    sh_o[0] = last
    z = pr + (prev - pr) * mu_ref[...]

    r = z[:, 0:WIDTH]
    k = z[:, WIDTH:2 * WIDTH]
    v = z[:, 2 * WIDTH:3 * WIDTH]
    dwa = z[:, LORA_OFF:LORA_OFF + LANES]
    dg = z[:, LORA_OFF + LANES:RWKV_COLS]
    w_log = -_softplus(-(w0_ref[...] + _dot(jnp.tanh(dwa).astype(BF16), wdu_ref[...]))) - 0.5
    decay = jnp.exp(-jnp.exp(w_log))
    a = jax.nn.sigmoid(a0_ref[...] + _dot(dwa.astype(BF16), wau_ref[...]))
    g = _dot(jax.nn.sigmoid(dg).astype(BF16), wgu_ref[...])
    kk = k * kk_ref[...]
    kk = kk / jnp.maximum(jnp.sqrt(_split2_dot(kk * kk, bd)), 1e-12)
    k2 = k * (1.0 + (a - 1.0) * ka_ref[...])
    bonus = _split2_dot(r * k2 * rk_ref[...], bd) * v

    r_o[0] = r
    w_o[0] = decay
    k_o[0] = k2
    v_o[0] = v
    kk_o[0] = kk
    b_o[0] = kk * a
    g_o[0] = g
    bonus_o[0] = bonus

    q = pf[:, 0:WIDTH]
    kf = pf[:, WIDTH:2 * WIDTH]
    vf = pf[:, 2 * WIDTH:3 * WIDTH]
    inv_hd = 1.0 / HEAD_DIM
    qn = q * lax.rsqrt(_split2_dot(q * q, bd) * inv_hd + NORM_EPS) * qg_ref[...]
    kn = kf * lax.rsqrt(_split2_dot(kf * kf, bd) * inv_hd + NORM_EPS) * kg_ref[...]
    kf_o[0] = kn
    vf_o[0] = vf
    logf = -_softplus(-(pfl + fb_ref[...]))
    lf_o[0] = logf

    @pl.when(t == 0)
    def _():
        lc_ref[...] = lcin_ref[0]

    head_lane = lax.broadcasted_iota(jnp.int32, (1, LANES), 1) < N_HEADS
    lc = _split3_dot_rhs(tri_ref[...], jnp.where(head_lane, logf, 0.0)) + lc_ref[...]
    lc_ref[...] = lc[tm - 1:tm, :]
    lc_hi = lc.astype(BF16)
    lc_r1 = lc - lc_hi.astype(F32)
    lc_mid = lc_r1.astype(BF16)
    lc_lo = (lc_r1 - lc_mid.astype(F32)).astype(BF16)

    q_bf = (qn * (HEAD_DIM ** -0.5)).astype(BF16)
    q_all = _dot(jnp.concatenate([q_bf, lc_hi, lc_mid, lc_lo], axis=-1), pq_ref[...]) + oneq_ref[...]
    k_all = _dot(jnp.concatenate([kn.astype(BF16), lc_hi, lc_mid, lc_lo], axis=-1), pk_ref[...]) + onek_ref[...]
    if tm < LANES:
        vf = jnp.concatenate([vf, jnp.zeros((LANES - tm, WIDTH), F32)], axis=0)
    v_t = vf.T[:, 0:tm].astype(BF16)
    for hd in range(N_HEADS):
        qa_o[0, hd] = q_all[:, hd * LANES:(hd + 1) * LANES].astype(BF16)
        ka_o[0, hd] = k_all[:, hd * LANES:(hd + 1) * LANES].astype(BF16)
        vt_o[0, hd] = v_t[hd * HEAD_DIM:(hd + 1) * HEAD_DIM, :]


def _placement(sign_first):
    rows = jnp.arange(WIDTH + 3 * LANES)[:, None]
    cols = jnp.arange(N_HEADS * LANES)[None, :]
    c_head, c_lane = cols // LANES, cols % LANES
    feat = (rows < WIDTH) & (rows // HEAD_DIM == c_head) & (rows % HEAD_DIM == c_lane)
    piece, p_head = (rows - WIDTH) // LANES, (rows - WIDTH) % LANES
    lc_lane0 = HEAD_DIM if sign_first else HEAD_DIM + 3
    bias = (rows >= WIDTH) & (p_head == c_head) & (c_lane == lc_lane0 + piece)
    place = feat.astype(F32) + bias.astype(F32) * (1.0 if sign_first else -1.0)
    one_lane0 = HEAD_DIM + 3 if sign_first else HEAD_DIM
    ones = ((c_lane >= one_lane0) & (c_lane < one_lane0 + 3)).astype(F32)
    return place.astype(BF16), ones


def _inproj(x, modp, shift_prev, lc_in, wts, tm):
    b, t, _ = x.shape
    nt = t // tm
    tok = lambda w, dt: jax.ShapeDtypeStruct((b, t, w), dt)
    tile = lambda w: pl.BlockSpec((1, tm, w), lambda i, j: (i, j, 0))
    consts = [wts[n] for n in ("norm1_g", "w_r", "w_f", "w_fl")]
    consts2 = [wts[n] for n in ("shift_mu", "w0", "a0", "w_du", "w_au", "w_gu", "k_k", "k_a", "r_k",
                                "fox_q_g", "fox_k_g", "fox_f_b", "bd")]
    tri = (jnp.arange(tm)[:, None] >= jnp.arange(tm)[None, :]).astype(BF16)
    place_q, ones_q = _placement(True)
    place_k, ones_k = _placement(False)
    consts3 = [tri, place_q, place_k, ones_q, ones_k]
    per_stream = lambda w: pl.BlockSpec((1, 1, w), lambda i, j: (i, 0, 0))
    in_specs = ([tile(D_MODEL), pl.BlockSpec((1, 6, D_MODEL), lambda i, j: (i, 0, 0))]
                + [_const_spec(c.shape) for c in consts]
                + [per_stream(RWKV_COLS), per_stream(LANES)]
                + [_const_spec(c.shape) for c in consts2 + consts3])
    head_major = jax.ShapeDtypeStruct((b, N_HEADS, t, LANES), BF16)
    head_tile = pl.BlockSpec((1, N_HEADS, tm, LANES), lambda i, j: (i, 0, j, 0))
    out_shape = ([tok(WIDTH, F32)] * 8
                 + [head_major, head_major, jax.ShapeDtypeStruct((b, N_HEADS, HEAD_DIM, t), BF16)]
                 + [tok(WIDTH, F32)] * 2 + [tok(LANES, F32), jax.ShapeDtypeStruct((b, 1, RWKV_COLS), F32)])
    out_specs = ([tile(WIDTH)] * 8
                 + [head_tile, head_tile, pl.BlockSpec((1, N_HEADS, HEAD_DIM, tm), lambda i, j: (i, 0, 0, j))]
                 + [tile(WIDTH)] * 2 + [tile(LANES), per_stream(RWKV_COLS)])
    return pl.pallas_call(
        functools.partial(_inproj_kernel, tm=tm),
        out_shape=out_shape,
        grid=(b, nt),
        in_specs=in_specs,
        out_specs=out_specs,
        scratch_shapes=[pltpu.VMEM((1, RWKV_COLS), F32), pltpu.VMEM((1, LANES), F32)],
        compiler_params=_cparams(("arbitrary", "arbitrary")),
        name="inproj",
    )(x, modp, *consts, shift_prev, lc_in, *consts2, *consts3)


def _exact_pieces(x):
    hi = x.astype(BF16).astype(F32)
    r1 = x - hi
    mid = r1.astype(BF16).astype(F32)
    return hi, mid, r1 - mid


def _scan_kernel(r_ref, w_ref, k_ref, v_ref, kk_ref, b_ref, s0_ref, sel_ref, e_ref, y_ref, s_ref,
                 lhs_ref, tile_ref, gam_ref, *, n_steps):
    t = pl.program_id(1)

    @pl.when(t == 0)
    def _():
        s_ref[...] = s0_ref[...]

    sel = sel_ref[...]
    e_all = e_ref[...]
    row_id = lax.broadcasted_iota(jnp.int32, (SUBLANES, LANES), 0)
    lo8 = lax.broadcasted_iota(jnp.int32, (SUBLANES, LANES), 1) < HEAD_DIM
    q_pad = jnp.zeros((LANES - 6 * SUBLANES, LANES), F32)

    def transposed_halves(x8):
        blocks = []
        for h in range(2):
            blocks.extend(_exact_pieces(jnp.where(lo8 if h == 0 else jnp.logical_not(lo8), x8, 0.0)))
        q = jnp.concatenate(blocks + [q_pad], axis=0).astype(BF16)
        return lax.dot_general(sel, q, (((1,), (1,)), ((), ())), preferred_element_type=F32).astype(BF16)

    def shift_rows(x, sh, fill):
        return jnp.where(row_id >= sh, pltpu.roll(x, sh, axis=0), fill)

    n_groups = n_steps // SUBLANES
    last_lanes = (SUBLANES - 1) * LANES

    tile_rows = N_PAIRS * 4 * HEAD_DIM

    def transpose_group(group, slot):
        base = pl.multiple_of(jnp.minimum(group, n_groups - 1) * SUBLANES, SUBLANES)
        for p in range(N_PAIRS):
            rows8 = lambda ref: ref[0, pl.ds(base, SUBLANES), pl.ds(p * LANES, LANES)]
            g_inc = rows8(w_ref)
            for sh in (1, 2, 4):
                g_inc = g_inc * shift_rows(g_inc, sh, 1.0)
            g_exc = shift_rows(g_inc, 1, 1.0)
            inv = 1.0 / g_inc
            scaled = (rows8(kk_ref) * g_exc, rows8(b_ref) * inv, rows8(k_ref) * inv, rows8(r_ref) * g_inc)
            for vec, x in enumerate(scaled):
                row0 = (p * 4 + vec) * HEAD_DIM
                lhs_ref[slot, row0:row0 + HEAD_DIM, :] = transposed_halves(x)
            row0 = tile_rows + p * HEAD_DIM
            lhs_ref[slot, row0:row0 + HEAD_DIM, :] = transposed_halves(g_inc)

    def expand_group(slot_in, slot_out):
        tile_ref[slot_out] = _dot(lhs_ref[slot_in, 0:tile_rows, :], e_all)
        gam_ref[slot_out] = _dot(lhs_ref[slot_in, tile_rows:tile_rows + N_PAIRS * HEAD_DIM, :],
                                 e_all[:, last_lanes:last_lanes + LANES])

    def run_tokens(group, slot, states):
        base = pl.multiple_of(group * SUBLANES, SUBLANES)
        states = list(states)
        v8 = [v_ref[0, pl.ds(base, SUBLANES), pl.ds(p * LANES, LANES)] for p in range(N_PAIRS)]
        ys = [[] for _ in range(N_PAIRS)]
        for u in range(SUBLANES):
            for p in range(N_PAIRS):
                def col(vec):
                    row0 = (p * 4 + vec) * HEAD_DIM
                    return tile_ref[slot, row0:row0 + HEAD_DIM, u * LANES:(u + 1) * LANES]
                s = states[p]
                sa = jnp.sum(s * col(0), axis=0, keepdims=True)
                s = s + col(2) * v8[p][u:u + 1, :] - col(1) * sa
                ys[p].append(jnp.sum(s * col(3), axis=0, keepdims=True))
                states[p] = s
        for p in range(N_PAIRS):
            states[p] = states[p] * gam_ref[slot, p * HEAD_DIM:(p + 1) * HEAD_DIM, :]
            y_ref[0, pl.ds(base, SUBLANES), pl.ds(p * LANES, LANES)] = jnp.concatenate(ys[p], axis=0)
        return tuple(states)

    def body(jj, states):
        g0 = 2 * jj
        states = run_tokens(g0, 0, states)
        expand_group(1, 1)
        transpose_group(g0 + 2, 0)
        states = run_tokens(g0 + 1, 1, states)
        expand_group(0, 0)
        transpose_group(g0 + 3, 1)
        return states

    transpose_group(0, 0)
    expand_group(0, 0)
    transpose_group(1, 1)
    init = tuple(s_ref[0, p] for p in range(N_PAIRS))
    final = lax.fori_loop(0, n_groups // 2, body, init)
    for p in range(N_PAIRS):
        s_ref[0, p] = final[p]


def _rwkv_scan(r, w, k, v, kk, bvec, s0_pairs):
    b, t, _ = r.shape
    chunk = min(SCAN_CHUNK, t)
    assert t % chunk == 0 and chunk % SUBLANES == 0
    nt = t // chunk
    lane = jnp.arange(LANES)
    sel = (lane[None, :] % HEAD_DIM == jnp.arange(HEAD_DIM)[:, None]).astype(BF16)
    q_row = jnp.arange(LANES)
    q_half, q_tok, q_used = q_row // (3 * SUBLANES), q_row % SUBLANES, q_row < 6 * SUBLANES
    col = jnp.arange(SUBLANES * LANES)
    e_all = (q_used[:, None] & (q_tok[:, None] == col[None, :] // LANES)
             & (q_half[:, None] == (col[None, :] % LANES) // HEAD_DIM)).astype(BF16)
    tile = pl.BlockSpec((1, chunk, WIDTH), lambda i, j: (i, j, 0))
    st = pl.BlockSpec((1, N_PAIRS, HEAD_DIM, LANES), lambda i, j: (i, 0, 0, 0))
    return pl.pallas_call(
        functools.partial(_scan_kernel, n_steps=chunk),
        out_shape=[jax.ShapeDtypeStruct((b, t, WIDTH), F32),
                   jax.ShapeDtypeStruct((b, N_PAIRS, HEAD_DIM, LANES), F32)],
        grid=(b, nt),
        in_specs=[tile] * 6 + [st, _const_spec(sel.shape), _const_spec(e_all.shape)],
        out_specs=[tile, st],
        scratch_shapes=[pltpu.VMEM((2, N_PAIRS * 5 * HEAD_DIM, LANES), BF16),
                        pltpu.VMEM((2, N_PAIRS * 4 * HEAD_DIM, SUBLANES * LANES), F32),
                        pltpu.VMEM((2, N_PAIRS * HEAD_DIM, LANES), F32)],
        compiler_params=_cparams(("arbitrary", "arbitrary")),
        name="rwkv_scan",
    )(r, w, k, v, kk, bvec, s0_pairs, sel, e_all)


def _cumsum_kernel(x_ref, u_ref, l_ref, o_ref):
    x = x_ref[0]
    local = _split3_dot_lhs(x, u_ref[...])
    tot = jnp.broadcast_to(local[:, LANES - 1:LANES], local.shape)
    o_ref[0] = local + _split3_dot_rhs(l_ref[...], tot)


def _cumsum(logf_t):
    b, h, t = logf_t.shape
    nb = t // LANES
    rows = h * nb
    x = logf_t.reshape(b, rows, LANES)
    col = jnp.arange(LANES)
    u = (col[:, None] <= col[None, :]).astype(BF16)
    rid = jnp.arange(rows)
    l = ((rid[:, None] // nb == rid[None, :] // nb) & (rid[None, :] < rid[:, None])).astype(BF16)
    out = pl.pallas_call(
        _cumsum_kernel,
        out_shape=jax.ShapeDtypeStruct((b, rows, LANES), F32),
        grid=(b,),
        in_specs=[pl.BlockSpec((1, rows, LANES), lambda i: (i, 0, 0)),
                  _const_spec(u.shape), _const_spec(l.shape)],
        out_specs=pl.BlockSpec((1, rows, LANES), lambda i: (i, 0, 0)),
        compiler_params=_cparams(("arbitrary",)),
        name="logf_cumsum",
    )(x, u, l)
    return out.reshape(b, h, t)


N_BIAS = 6


def _attn_kernel(q_ref, k_ref, vt_ref, o_ref, m_ref, l_ref, acc_ref, *, tq, tk, past):
    qi = pl.program_id(2)
    q_first = past + qi * tq
    n_full = q_first // tk

    m_ref[...] = jnp.full_like(m_ref, -jnp.inf)
    l_ref[...] = jnp.zeros_like(l_ref)
    acc_ref[...] = jnp.zeros_like(acc_ref)

    def kv_block(j, masked):
        k_start = pl.multiple_of(j * tk, tk)
        for hb in range(2):
            s = lax.dot_general(k_ref[0, hb, pl.ds(k_start, tk), :], q_ref[0, hb], (((1,), (1,)), ((), ())),
                                preferred_element_type=F32)
            if masked:
                k_pos = k_start + lax.broadcasted_iota(jnp.int32, (tk, tq), 0)
                q_pos = q_first + lax.broadcasted_iota(jnp.int32, (tk, tq), 1)
                s = jnp.where(k_pos <= q_pos, s, -jnp.inf)
            m_prev = m_ref[hb]
            m_new = jnp.maximum(m_prev, jnp.max(s, axis=0, keepdims=True))
            alpha = jnp.exp(m_prev - m_new)
            p = jnp.exp(s - m_new)
            l_ref[hb] = alpha * l_ref[hb] + jnp.sum(p, axis=0, keepdims=True)
            m_ref[hb] = m_new
            acc_ref[hb] = alpha * acc_ref[hb] + _dot(vt_ref[0, hb, :, pl.ds(k_start, tk)], p.astype(BF16))

    def full_block(j, carry):
        kv_block(j, False)
        return carry

    lax.fori_loop(0, n_full, full_block, 0)
    kv_block(n_full, True)
    out_t = jnp.concatenate([acc_ref[0] / l_ref[0], acc_ref[1] / l_ref[1]], axis=0)
    o_ref[0] = out_t.T.astype(o_ref.dtype)


def _past_keys_aug(k_past, lc_past):
    b, p = k_past.shape[:2]
    hi = lc_past.astype(BF16)
    r1 = lc_past - hi.astype(F32)
    mid = r1.astype(BF16)
    lo = (r1 - mid.astype(F32)).astype(BF16)
    ones = jnp.ones_like(hi)
    bias = jnp.stack([ones, ones, ones, -hi, -mid, -lo], axis=-1)
    feats = jnp.transpose(k_past.astype(BF16), (0, 2, 1, 3))
    pad = jnp.zeros((b, N_HEADS, p, LANES - HEAD_DIM - N_BIAS), BF16)
    return jnp.concatenate([feats, bias, pad], axis=-1)


def _fox_attention(q_aug, k_aug, v_t, past, tq, tk):
    b, _, t_q, _ = q_aug.shape
    t_k = k_aug.shape[2]
    assert t_q % tq == 0 and t_k % tk == 0 and tk % tq == 0 and past % tq == 0
    return pl.pallas_call(
        functools.partial(_attn_kernel, tq=tq, tk=tk, past=past),
        out_shape=jax.ShapeDtypeStruct((b, t_q, WIDTH), BF16),
        grid=(b, N_PAIRS, t_q // tq),
        in_specs=[pl.BlockSpec((1, 2, tq, LANES), lambda i, p, qi: (i, p, qi, 0)),
                  pl.BlockSpec((1, 2, t_k, LANES), lambda i, p, qi: (i, p, 0, 0)),
                  pl.BlockSpec((1, 2, HEAD_DIM, t_k), lambda i, p, qi: (i, p, 0, 0))],
        out_specs=pl.BlockSpec((1, tq, LANES), lambda i, p, qi: (i, qi, p)),
        scratch_shapes=[pltpu.VMEM((2, 1, tq), F32), pltpu.VMEM((2, 1, tq), F32),
                        pltpu.VMEM((2, HEAD_DIM, tq), F32)],
        compiler_params=_cparams(("arbitrary", "arbitrary", "arbitrary")),
        name="fox_attention",
    )(q_aug, k_aug, v_t)


def _out_kernel(x_ref, ys_ref, g_ref, bonus_ref, yf_ref, mod_ref, gng_ref, gnb_ref, bd_ref, wo_ref, n2_ref,
                wg_ref, wu_ref, wd_ref, o_ref):
    bd = bd_ref[...]
    gt1 = mod_ref[0, 2:3, :]
    sh2 = mod_ref[0, 3:4, :]
    sc2 = mod_ref[0, 4:5, :]
    gt2 = mod_ref[0, 5:6, :]
    inv_hd = 1.0 / HEAD_DIM
    ys = ys_ref[0]
    d = ys - _split2_dot(ys, bd) * inv_hd
    var = _split2_dot(d * d, bd) * inv_hd
    yn = d * lax.rsqrt(var + GN_EPS) * gng_ref[...] + gnb_ref[...]
    y_rwkv = ((yn + bonus_ref[0]) * g_ref[0]).astype(BF16)
    mix = _dot(y_rwkv, wo_ref[0:WIDTH, :]) + _dot(yf_ref[0], wo_ref[WIDTH:2 * WIDTH, :])
    x1 = x_ref[0] + gt1 * mix
    h2 = x1 * lax.rsqrt(jnp.mean(x1 * x1, axis=-1, keepdims=True) + NORM_EPS) * n2_ref[...]
    hb = (h2 * (1.0 + sc2) + sh2).astype(BF16)
    act = (_silu(_dot(hb, wg_ref[...])) * _dot(hb, wu_ref[...])).astype(BF16)
    o_ref[0] = x1 + gt2 * _dot(act, wd_ref[...])


def _outproj_ffn(x, ys, g, bonus, yf, modp, wts, tm):
    b, t, _ = x.shape
    tile = lambda w: pl.BlockSpec((1, tm, w), lambda i, j: (i, j, 0))
    consts = [wts[n] for n in ("gn_g", "gn_b", "bd", "w_out", "norm2_g", "w_ffn_gate", "w_ffn_up", "w_ffn_down")]
    return pl.pallas_call(
        _out_kernel,
        out_shape=jax.ShapeDtypeStruct((b, t, D_MODEL), F32),
        grid=(b, t // tm),
        in_specs=([tile(D_MODEL), tile(WIDTH), tile(WIDTH), tile(WIDTH), tile(WIDTH),
                   pl.BlockSpec((1, 6, D_MODEL), lambda i, j: (i, 0, 0))]
                  + [_const_spec(c.shape) for c in consts]),
        out_specs=tile(D_MODEL),
        compiler_params=_cparams(("arbitrary", "arbitrary")),
        name="outproj_ffn",
    )(x, ys, g, bonus, yf, modp, *consts)


def _prep_weights(norm1_g, w_in, shift_mu, w0, w_decay_up, a0, w_aaa_up, w_gate_up, k_k, k_a, r_k, gn_g, gn_b,
                  fox_q_g, fox_k_g, fox_f_b, w_out, norm2_g, w_ffn_gate, w_ffn_up, w_ffn_down):
    row = lambda a: a.reshape(1, -1).astype(F32)
    fox0 = RWKV_COLS
    zeros_lora = jnp.zeros((DECAY_LORA, WIDTH), BF16)
    head = jnp.arange(WIDTH) // HEAD_DIM
    return {
        "norm1_g": row(norm1_g),
        "w_r": w_in[:, :RWKV_COLS].astype(BF16),
        "w_f": w_in[:, fox0:fox0 + 3 * WIDTH].astype(BF16),
        "w_fl": jnp.pad(w_in[:, fox0 + 3 * WIDTH:], ((0, 0), (0, LANES - N_HEADS))).astype(BF16),
        "shift_mu": row(shift_mu), "w0": row(w0), "a0": row(a0),
        "w_du": jnp.concatenate([w_decay_up.astype(BF16), zeros_lora], axis=0),
        "w_au": jnp.concatenate([zeros_lora, w_aaa_up.astype(BF16)], axis=0),
        "w_gu": w_gate_up.astype(BF16),
        "k_k": row(k_k), "k_a": row(k_a), "r_k": row(r_k),
        "fox_q_g": row(jnp.tile(fox_q_g, N_HEADS)), "fox_k_g": row(jnp.tile(fox_k_g, N_HEADS)),
        "fox_f_b": jnp.pad(row(fox_f_b), ((0, 0), (0, LANES - N_HEADS))),
        "bd": (head[:, None] == head[None, :]).astype(BF16),
        "gn_g": row(gn_g), "gn_b": row(gn_b),
        "w_out": w_out.astype(BF16), "norm2_g": row(norm2_g),
        "w_ffn_gate": w_ffn_gate.astype(BF16), "w_ffn_up": w_ffn_up.astype(BF16),
        "w_ffn_down": w_ffn_down.astype(BF16),
    }


def _to_pairs(state):
    b = state.shape[0]
    s = state.reshape(b, N_PAIRS, 2, HEAD_DIM, HEAD_DIM)
    return jnp.transpose(s, (0, 1, 4, 2, 3)).reshape(b, N_PAIRS, HEAD_DIM, LANES)


def _from_pairs(state):
    b = state.shape[0]
    s = state.reshape(b, N_PAIRS, HEAD_DIM, 2, HEAD_DIM)
    return jnp.transpose(s, (0, 1, 3, 4, 2)).reshape(b, N_HEADS, HEAD_DIM, HEAD_DIM)


def _round_up(n, m):
    return -(-n // m) * m


def _layer(x, modp, shift_prev, s_prev, k_past, v_past, logf_past, wts, tm, tq):
    b, t, _ = x.shape
    past = k_past.shape[1]
    if past:
        lc_past = _cumsum(jnp.transpose(logf_past.astype(F32), (0, 2, 1)))
        lc_in = jnp.pad(lc_past[:, None, :, past - 1], ((0, 0), (0, 0), (0, LANES - N_HEADS)))
    else:
        lc_in = jnp.zeros((b, 1, LANES), F32)
    (r, w, k2, v, kk, bvec, g, bonus, q_aug, k_aug, v_t, kf, vf, lf_pad, new_shift) = _inproj(
        x, modp, shift_prev, lc_in, wts, tm)

    ys, s_new = _rwkv_scan(r, w, k2, v, kk, bvec, _to_pairs(s_prev.astype(F32)))

    logf = lf_pad[:, :, :N_HEADS]
    tq = min(tq, t)
    if past:
        tk_all = _round_up(past + t, LANES)
        tail = tk_all - past - t
        k_aug = jnp.concatenate([_past_keys_aug(k_past, lc_past), k_aug,
                                 jnp.zeros((b, N_HEADS, tail, LANES), BF16)], axis=2)
        v_t = jnp.concatenate([jnp.transpose(v_past.astype(BF16), (0, 2, 3, 1)), v_t,
                               jnp.zeros((b, N_HEADS, HEAD_DIM, tail), BF16)], axis=3)
        tk = tk_all if t == tq else tq
    else:
        tk = tq
    yf = _fox_attention(q_aug, k_aug, v_t, past, tq, tk)

    y = _outproj_ffn(x, ys, g, bonus, yf, modp, wts, tm)
    return (y, _from_pairs(s_new), new_shift, kf.reshape(b, t, N_HEADS, HEAD_DIM),
            vf.reshape(b, t, N_HEADS, HEAD_DIM), logf)


def kernel(x_prompt, x_sample, cache_fox_k, cache_fox_v, cache_fox_logf, state_rwkv, state_rwkv_shift, c_prompt, c_sample, norm1_g, w_ada, b_ada, w_in, shift_mu, w0, w_decay_up, a0, w_aaa_up, w_gate_up, k_k, k_a, r_k, gn_g, gn_b, fox_q_g, fox_k_g, fox_f_b, w_out, norm2_g, w_ffn_gate, w_ffn_up, w_ffn_down):
    depth = w_in.shape[0]
    bp, bs = x_prompt.shape[0], x_sample.shape[0]
    zero_shift = jnp.zeros((bp, 1, RWKV_COLS), F32)
    zero_state = jnp.zeros((bp, N_HEADS, HEAD_DIM, HEAD_DIM), F32)
    zero_kv = jnp.zeros((bp, 0, N_HEADS, HEAD_DIM), F32)
    zero_logf = jnp.zeros((bp, 0, N_HEADS), F32)
    hp, hs = x_prompt, x_sample
    outs_p, outs_s = [], []
    for l in range(depth):
        wts = _prep_weights(norm1_g[l], w_in[l], shift_mu[l], w0[l], w_decay_up[l], a0[l], w_aaa_up[l],
                            w_gate_up[l], k_k[l], k_a[l], r_k[l], gn_g[l], gn_b[l], fox_q_g[l], fox_k_g[l],
                            fox_f_b[l], w_out[l], norm2_g[l], w_ffn_gate[l], w_ffn_up[l], w_ffn_down[l])
        rows = _round_up(bp + bs, 8)
        c_all = jnp.pad(jnp.concatenate([c_prompt, c_sample], axis=0), ((0, rows - bp - bs), (0, 0)))
        mod = _adaln(c_all, w_ada[l].astype(BF16), b_ada[l].reshape(1, -1)).reshape(rows, 6, D_MODEL)
        res_p = _layer(hp, mod[:bp], zero_shift, zero_state, zero_kv, zero_kv, zero_logf, wts, tm=256, tq=512)
        res_s = _layer(hs, mod[bp:bp + bs], state_rwkv_shift[l], state_rwkv[l], cache_fox_k[l], cache_fox_v[l],
                       cache_fox_logf[l], wts, tm=64, tq=512)
        hp, hs = res_p[0], res_s[0]
        outs_p.append(res_p[1:])
        outs_s.append(res_s[1:])
    stack = lambda outs, i: jnp.stack([o[i] for o in outs])
    return (hp, hs,
            stack(outs_p, 0), stack(outs_p, 1), stack(outs_p, 2), stack(outs_p, 3), stack(outs_p, 4),
            stack(outs_s, 0), stack(outs_s, 1), stack(outs_s, 2), stack(outs_s, 3), stack(outs_s, 4))
```

```python
import functools

import jax
import jax.numpy as jnp
from jax import lax
from jax.experimental import pallas as pl
from jax.experimental.pallas import tpu as pltpu

F32 = jnp.float32
BF16 = jnp.bfloat16

D_MODEL = 1024
HEAD_DIM = 64
N_HEADS = 8
WIDTH = N_HEADS * HEAD_DIM
N_PAIRS = N_HEADS // 2
DECAY_LORA = 64
AAA_LORA = 64
GATE_LORA = 128
RWKV_COLS = 3 * WIDTH + DECAY_LORA + AAA_LORA + GATE_LORA
LORA_OFF = 3 * WIDTH
D_FF = 2816
NORM_EPS = 1e-6
GN_EPS = 64e-5
LANES = 128
SUBLANES = 8
SCAN_CHUNK = 512
N_BIAS = 6
V_ROWS = 80
LOG2E = 1.4426950408889634
VMEM_LIMIT = 56 * 1024 * 1024


def _cparams(sem):
    return pltpu.CompilerParams(dimension_semantics=sem, vmem_limit_bytes=VMEM_LIMIT)


def _const_spec(shape):
    nd = len(shape)
    return pl.BlockSpec(shape, lambda *_: (0,) * nd, pipeline_mode=pl.Buffered(1))


def _softplus(x):
    return jnp.maximum(x, 0.0) + jnp.log1p(jnp.exp(-jnp.abs(x)))


def _silu(x):
    return x * jax.nn.sigmoid(x)


def _dot(a, b):
    return jnp.dot(a, b, preferred_element_type=F32)


def _split2_dot(x, m):
    hi = x.astype(BF16)
    lo = (x - hi.astype(F32)).astype(BF16)
    half = m.shape[0]
    parts = [_dot(hi[:, c:c + half], m) + _dot(lo[:, c:c + half], m) for c in range(0, x.shape[1], half)]
    return jnp.concatenate(parts, axis=-1)


def _split3_dot_rhs(m, x):
    hi = x.astype(BF16)
    r1 = x - hi.astype(F32)
    mid = r1.astype(BF16)
    lo = (r1 - mid.astype(F32)).astype(BF16)
    return _dot(m, hi) + _dot(m, mid) + _dot(m, lo)


def _split3_dot_lhs(x, m):
    hi = x.astype(BF16)
    r1 = x - hi.astype(F32)
    mid = r1.astype(BF16)
    lo = (r1 - mid.astype(F32)).astype(BF16)
    return _dot(hi, m) + _dot(mid, m) + _dot(lo, m)


def _mod_kernel(c_ref, w_ref, b_ref, o_ref):
    o_ref[...] = _dot(_silu(c_ref[...]).astype(BF16), w_ref[...]) + b_ref[...]


def _adaln(c_all, w_ada_bf, b_ada):
    rows = c_all.shape[0]
    n = w_ada_bf.shape[1]
    bn = D_MODEL
    return pl.pallas_call(
        _mod_kernel,
        out_shape=jax.ShapeDtypeStruct((rows, n), F32),
        grid=(n // bn,),
        in_specs=[pl.BlockSpec((rows, D_MODEL), lambda j: (0, 0)),
                  pl.BlockSpec((D_MODEL, bn), lambda j: (0, j)),
                  pl.BlockSpec((1, bn), lambda j: (0, j))],
        out_specs=pl.BlockSpec((rows, bn), lambda j: (0, j)),
        compiler_params=_cparams(("arbitrary",)),
        name="adaln_mod",
    )(c_all, w_ada_bf, b_ada)


def _inproj_kernel(x_ref, mod_ref, g1_ref, wr_ref, wf_ref, wfl_ref, sprev_ref, lcin_ref, mu_ref, w0_ref, a0_ref,
                   wdu_ref, wau_ref, wgu_ref, kk_ref, ka_ref, rk_ref, qg_ref, kg_ref, fb_ref, bd_ref,
                   tri_ref, pq_ref, pk_ref, oneq_ref, onek_ref,
                   r_o, w_o, k_o, v_o, kk_o, b_o, g_o, bonus_o, qa_o, ka_o, vt_o, kf_o, vf_o, lf_o, sh_o,
                   carry_ref, lc_ref, *, tm):
    t = pl.program_id(1)
    bd = bd_ref[...]
    x = x_ref[0]
    sh1 = mod_ref[0, 0:1, :]
    sc1 = mod_ref[0, 1:2, :]
    h = x * lax.rsqrt(jnp.mean(x * x, axis=-1, keepdims=True) + NORM_EPS) * g1_ref[...]
    hb = (h * (1.0 + sc1) + sh1).astype(BF16)
    pr = _dot(hb, wr_ref[...])
    pf = _dot(hb, wf_ref[...])
    pfl = _dot(hb, wfl_ref[...])

    @pl.when(t == 0)
    def _():
        carry_ref[...] = sprev_ref[0]

    prev = pltpu.roll(pr, 1, axis=0)
    first_row = lax.broadcasted_iota(jnp.int32, pr.shape, 0) == 0
    prev = jnp.where(first_row, carry_ref[...], prev)
    last = pr[tm - 1:tm, :]
    carry_ref[...] = last
    sh_o[0] = last
    z = pr + (prev - pr) * mu_ref[...]

    r = z[:, 0:WIDTH]
    k = z[:, WIDTH:2 * WIDTH]
    v = z[:, 2 * WIDTH:3 * WIDTH]
    dwa = z[:, LORA_OFF:LORA_OFF + LANES]
    dg = z[:, LORA_OFF + LANES:RWKV_COLS]
    w_log = -_softplus(-(w0_ref[...] + _dot(jnp.tanh(dwa).astype(BF16), wdu_ref[...]))) - 0.5
    decay = jnp.exp(-jnp.exp(w_log))
    a = jax.nn.sigmoid(a0_ref[...] + _dot(dwa.astype(BF16), wau_ref[...]))
    g = _dot(jax.nn.sigmoid(dg).astype(BF16), wgu_ref[...])
    kk = k * kk_ref[...]
    kk = kk / jnp.maximum(jnp.sqrt(_split2_dot(kk * kk, bd)), 1e-12)
    k2 = k * (1.0 + (a - 1.0) * ka_ref[...])
    bonus = _split2_dot(r * k2 * rk_ref[...], bd) * v

    r_o[0] = r
    w_o[0] = decay
    k_o[0] = k2
    v_o[0] = v
    kk_o[0] = kk
    b_o[0] = kk * a
    g_o[0] = g
    bonus_o[0] = bonus

    q = pf[:, 0:WIDTH]
    kf = pf[:, WIDTH:2 * WIDTH]
    vf = pf[:, 2 * WIDTH:3 * WIDTH]
    inv_hd = 1.0 / HEAD_DIM
    qn = q * lax.rsqrt(_split2_dot(q * q, bd) * inv_hd + NORM_EPS) * qg_ref[...]
    kn = kf * lax.rsqrt(_split2_dot(kf * kf, bd) * inv_hd + NORM_EPS) * kg_ref[...]
    kf_o[0] = kn
    vf_o[0] = vf
    logf = -_softplus(-(pfl + fb_ref[...]))
    lf_o[0] = logf

    @pl.when(t == 0)
    def _():
        lc_ref[...] = lcin_ref[0]

    head_lane = lax.broadcasted_iota(jnp.int32, (1, LANES), 1) < N_HEADS
    lc = _split3_dot_rhs(tri_ref[...], jnp.where(head_lane, logf, 0.0)) + lc_ref[...]
    lc_ref[...] = lc[tm - 1:tm, :]
    lc = lc * LOG2E
    lc_hi = lc.astype(BF16)
    lc_r1 = lc - lc_hi.astype(F32)
    lc_mid = lc_r1.astype(BF16)
    lc_lo = (lc_r1 - lc_mid.astype(F32)).astype(BF16)

    q_bf = (qn * (HEAD_DIM ** -0.5 * LOG2E)).astype(BF16)
    q_all = _dot(jnp.concatenate([q_bf, lc_hi, lc_mid, lc_lo], axis=-1), pq_ref[...]) + oneq_ref[...]
    k_all = _dot(jnp.concatenate([kn.astype(BF16), lc_hi, lc_mid, lc_lo], axis=-1), pk_ref[...]) + onek_ref[...]
    if tm < LANES:
        vf = jnp.concatenate([vf, jnp.zeros((LANES - tm, WIDTH), F32)], axis=0)
    v_t = vf.T[:, 0:tm].astype(BF16)
    extra_rows = lax.broadcasted_iota(jnp.int32, (V_ROWS - HEAD_DIM, tm), 0) == 0
    ones_then_zeros = jnp.where(extra_rows, 1.0, 0.0).astype(BF16)
    for hd in range(N_HEADS):
        qa_o[0, hd] = q_all[:, hd * LANES:(hd + 1) * LANES].astype(BF16)
        ka_o[0, hd] = k_all[:, hd * LANES:(hd + 1) * LANES].astype(BF16)
        vt_o[0, hd, 0:HEAD_DIM, :] = v_t[hd * HEAD_DIM:(hd + 1) * HEAD_DIM, :]
        vt_o[0, hd, HEAD_DIM:V_ROWS, :] = ones_then_zeros


def _placement(sign_first):
    rows = jnp.arange(WIDTH + 3 * LANES)[:, None]
    cols = jnp.arange(N_HEADS * LANES)[None, :]
    c_head, c_lane = cols // LANES, cols % LANES
    feat = (rows < WIDTH) & (rows // HEAD_DIM == c_head) & (rows % HEAD_DIM == c_lane)
    piece, p_head = (rows - WIDTH) // LANES, (rows - WIDTH) % LANES
    lc_lane0 = HEAD_DIM if sign_first else HEAD_DIM + 3
    bias = (rows >= WIDTH) & (p_head == c_head) & (c_lane == lc_lane0 + piece)
    place = feat.astype(F32) + bias.astype(F32) * (1.0 if sign_first else -1.0)
    one_lane0 = HEAD_DIM + 3 if sign_first else HEAD_DIM
    ones = ((c_lane >= one_lane0) & (c_lane < one_lane0 + 3)).astype(F32)
    return place.astype(BF16), ones


def _inproj(x, modp, shift_prev, lc_in, wts, tm):
    b, t, _ = x.shape
    nt = t // tm
    tok = lambda w, dt: jax.ShapeDtypeStruct((b, t, w), dt)
    tile = lambda w: pl.BlockSpec((1, tm, w), lambda i, j: (i, j, 0))
    consts = [wts[n] for n in ("norm1_g", "w_r", "w_f", "w_fl")]
    consts2 = [wts[n] for n in ("shift_mu", "w0", "a0", "w_du", "w_au", "w_gu", "k_k", "k_a", "r_k",
                                "fox_q_g", "fox_k_g", "fox_f_b", "bd")]
    tri = (jnp.arange(tm)[:, None] >= jnp.arange(tm)[None, :]).astype(BF16)
    place_q, ones_q = _placement(True)
    place_k, ones_k = _placement(False)
    consts3 = [tri, place_q, place_k, ones_q, ones_k]
    per_stream = lambda w: pl.BlockSpec((1, 1, w), lambda i, j: (i, 0, 0))
    in_specs = ([tile(D_MODEL), pl.BlockSpec((1, 6, D_MODEL), lambda i, j: (i, 0, 0))]
                + [_const_spec(c.shape) for c in consts]
                + [per_stream(RWKV_COLS), per_stream(LANES)]
                + [_const_spec(c.shape) for c in consts2 + consts3])
    head_major = jax.ShapeDtypeStruct((b, N_HEADS, t, LANES), BF16)
    head_tile = pl.BlockSpec((1, N_HEADS, tm, LANES), lambda i, j: (i, 0, j, 0))
    out_shape = ([tok(WIDTH, F32)] * 8
                 + [head_major, head_major, jax.ShapeDtypeStruct((b, N_HEADS, V_ROWS, t), BF16)]
                 + [tok(WIDTH, F32)] * 2 + [tok(LANES, F32), jax.ShapeDtypeStruct((b, 1, RWKV_COLS), F32)])
    out_specs = ([tile(WIDTH)] * 8
                 + [head_tile, head_tile, pl.BlockSpec((1, N_HEADS, V_ROWS, tm), lambda i, j: (i, 0, 0, j))]
                 + [tile(WIDTH)] * 2 + [tile(LANES), per_stream(RWKV_COLS)])
    return pl.pallas_call(
        functools.partial(_inproj_kernel, tm=tm),
        out_shape=out_shape,
        grid=(b, nt),
        in_specs=in_specs,
        out_specs=out_specs,
        scratch_shapes=[pltpu.VMEM((1, RWKV_COLS), F32), pltpu.VMEM((1, LANES), F32)],
        compiler_params=_cparams(("arbitrary", "arbitrary")),
        name="inproj",
    )(x, modp, *consts, shift_prev, lc_in, *consts2, *consts3)


def _exact_pieces(x):
    hi = x.astype(BF16).astype(F32)
    r1 = x - hi
    mid = r1.astype(BF16).astype(F32)
    return hi, mid, r1 - mid


def _scan_kernel(r_ref, w_ref, k_ref, v_ref, kk_ref, b_ref, s0_ref, sel_ref, e_ref, y_ref, s_ref,
                 lhs_ref, tile_ref, gam_ref, *, n_steps):
    t = pl.program_id(1)

    @pl.when(t == 0)
    def _():
        s_ref[...] = s0_ref[...]

    sel = sel_ref[...]
    e_all = e_ref[...]
    row_id = lax.broadcasted_iota(jnp.int32, (SUBLANES, LANES), 0)
    lo8 = lax.broadcasted_iota(jnp.int32, (SUBLANES, LANES), 1) < HEAD_DIM
    n_vec = 5
    vec_cols = 6 * SUBLANES
    q_pad = jnp.zeros((2 * LANES - n_vec * vec_cols, LANES), F32)
    col_vec = lax.broadcasted_iota(jnp.int32, (1, 2 * LANES), 1) // vec_cols

    def transposed_halves(vectors):
        blocks = []
        for x8 in vectors:
            for h in range(2):
                blocks.extend(_exact_pieces(jnp.where(lo8 if h == 0 else jnp.logical_not(lo8), x8, 0.0)))
        q = jnp.concatenate(blocks + [q_pad], axis=0).astype(BF16)
        return lax.dot_general(sel, q, (((1,), (1,)), ((), ())), preferred_element_type=F32).astype(BF16)

    def shift_rows(x, sh, fill):
        return jnp.where(row_id >= sh, pltpu.roll(x, sh, axis=0), fill)

    n_groups = n_steps // SUBLANES
    last_lanes = (SUBLANES - 1) * LANES

    tile_rows = N_PAIRS * 4 * HEAD_DIM

    def transpose_group(group, slot):
        base = pl.multiple_of(jnp.minimum(group, n_groups - 1) * SUBLANES, SUBLANES)
        for p in range(N_PAIRS):
            rows8 = lambda ref: ref[0, pl.ds(base, SUBLANES), pl.ds(p * LANES, LANES)]
            g_inc = rows8(w_ref)
            for sh in (1, 2, 4):
                g_inc = g_inc * shift_rows(g_inc, sh, 1.0)
            g_exc = shift_rows(g_inc, 1, 1.0)
            inv = 1.0 / g_inc
            scaled = (rows8(kk_ref) * g_exc, rows8(b_ref) * inv, rows8(k_ref) * inv, rows8(r_ref) * g_inc, g_inc)
            lhs_all = transposed_halves(scaled)
            zero = jnp.zeros((), BF16)
            for vec in range(n_vec):
                row0 = (p * 4 + vec) * HEAD_DIM if vec < 4 else tile_rows + p * HEAD_DIM
                lhs_ref[slot, row0:row0 + HEAD_DIM, :] = jnp.where(col_vec == vec, lhs_all, zero)

    def expand_group(slot_in, slot_out):
        tile_ref[slot_out] = _dot(lhs_ref[slot_in, 0:tile_rows, :], e_all)
        gam_ref[slot_out] = _dot(lhs_ref[slot_in, tile_rows:tile_rows + N_PAIRS * HEAD_DIM, :],
                                 e_all[:, last_lanes:last_lanes + LANES])

    def run_tokens(group, slot, states):
        base = pl.multiple_of(group * SUBLANES, SUBLANES)
        states = list(states)
        v8 = [v_ref[0, pl.ds(base, SUBLANES), pl.ds(p * LANES, LANES)] for p in range(N_PAIRS)]
        ys = [[] for _ in range(N_PAIRS)]
        for u in range(SUBLANES):
            for p in range(N_PAIRS):
                def col(vec):
                    row0 = (p * 4 + vec) * HEAD_DIM
                    return tile_ref[slot, row0:row0 + HEAD_DIM, u * LANES:(u + 1) * LANES]
                s = states[p]
                sa = jnp.sum(s * col(0), axis=0, keepdims=True)
                s = s + col(2) * v8[p][u:u + 1, :] - col(1) * sa
                ys[p].append(jnp.sum(s * col(3), axis=0, keepdims=True))
                states[p] = s
        for p in range(N_PAIRS):
            states[p] = states[p] * gam_ref[slot, p * HEAD_DIM:(p + 1) * HEAD_DIM, :]
            y_ref[0, pl.ds(base, SUBLANES), pl.ds(p * LANES, LANES)] = jnp.concatenate(ys[p], axis=0)
        return tuple(states)

    def body(jj, states):
        g0 = 2 * jj
        states = run_tokens(g0, 0, states)
        expand_group(1, 1)
        transpose_group(g0 + 2, 0)
        states = run_tokens(g0 + 1, 1, states)
        expand_group(0, 0)
        transpose_group(g0 + 3, 1)
        return states

    transpose_group(0, 0)
    expand_group(0, 0)
    transpose_group(1, 1)
    init = tuple(s_ref[0, p] for p in range(N_PAIRS))
    final = lax.fori_loop(0, n_groups // 2, body, init)
    for p in range(N_PAIRS):
        s_ref[0, p] = final[p]


def _rwkv_scan(r, w, k, v, kk, bvec, s0_pairs):
    b, t, _ = r.shape
    chunk = min(SCAN_CHUNK, t)
    assert t % chunk == 0 and chunk % SUBLANES == 0
    nt = t // chunk
    lane = jnp.arange(LANES)
    sel = (lane[None, :] % HEAD_DIM == jnp.arange(HEAD_DIM)[:, None]).astype(BF16)
    q_row = jnp.arange(2 * LANES)
    q_half = (q_row % (6 * SUBLANES)) // (3 * SUBLANES)
    q_tok, q_used = q_row % SUBLANES, q_row < 5 * 6 * SUBLANES
    col = jnp.arange(SUBLANES * LANES)
    e_all = (q_used[:, None] & (q_tok[:, None] == col[None, :] // LANES)
             & (q_half[:, None] == (col[None, :] % LANES) // HEAD_DIM)).astype(BF16)
    tile = pl.BlockSpec((1, chunk, WIDTH), lambda i, j: (i, j, 0))
    st = pl.BlockSpec((1, N_PAIRS, HEAD_DIM, LANES), lambda i, j: (i, 0, 0, 0))
    return pl.pallas_call(
        functools.partial(_scan_kernel, n_steps=chunk),
        out_shape=[jax.ShapeDtypeStruct((b, t, WIDTH), F32),
                   jax.ShapeDtypeStruct((b, N_PAIRS, HEAD_DIM, LANES), F32)],
        grid=(b, nt),
        in_specs=[tile] * 6 + [st, _const_spec(sel.shape), _const_spec(e_all.shape)],
        out_specs=[tile, st],
        scratch_shapes=[pltpu.VMEM((2, N_PAIRS * 5 * HEAD_DIM, 2 * LANES), BF16),
                        pltpu.VMEM((2, N_PAIRS * 4 * HEAD_DIM, SUBLANES * LANES), F32),
                        pltpu.VMEM((2, N_PAIRS * HEAD_DIM, LANES), F32)],
        compiler_params=_cparams(("arbitrary", "arbitrary")),
        name="rwkv_scan",
    )(r, w, k, v, kk, bvec, s0_pairs, sel, e_all)


def _cumsum_kernel(x_ref, u_ref, l_ref, o_ref):
    x = x_ref[0]
    local = _split3_dot_lhs(x, u_ref[...])
    tot = jnp.broadcast_to(local[:, LANES - 1:LANES], local.shape)
    o_ref[0] = local + _split3_dot_rhs(l_ref[...], tot)


def _cumsum(logf_t):
    b, h, t = logf_t.shape
    nb = t // LANES
    rows = h * nb
    x = logf_t.reshape(b, rows, LANES)
    col = jnp.arange(LANES)
    u = (col[:, None] <= col[None, :]).astype(BF16)
    rid = jnp.arange(rows)
    l = ((rid[:, None] // nb == rid[None, :] // nb) & (rid[None, :] < rid[:, None])).astype(BF16)
    out = pl.pallas_call(
        _cumsum_kernel,
        out_shape=jax.ShapeDtypeStruct((b, rows, LANES), F32),
        grid=(b,),
        in_specs=[pl.BlockSpec((1, rows, LANES), lambda i: (i, 0, 0)),
                  _const_spec(u.shape), _const_spec(l.shape)],
        out_specs=pl.BlockSpec((1, rows, LANES), lambda i: (i, 0, 0)),
        compiler_params=_cparams(("arbitrary",)),
        name="logf_cumsum",
    )(x, u, l)
    return out.reshape(b, h, t)


def _attn_kernel(q_ref, k_ref, vt_ref, o_ref, m_ref, alpha_ref, p_ref, acc_ref, *, tq, tk, past):
    qi = pl.program_id(2)
    q_first = past + qi * tq
    n_full = q_first // tk

    m_ref[...] = jnp.full_like(m_ref, -jnp.inf)
    alpha_ref[...] = jnp.ones_like(alpha_ref)
    p_ref[...] = jnp.zeros_like(p_ref)
    acc_ref[...] = jnp.zeros_like(acc_ref)

    def add_values(j):
        k_start = pl.multiple_of(j * tk, tk)
        for hb in range(2):
            acc_ref[hb] = alpha_ref[hb] * acc_ref[hb] + _dot(vt_ref[0, hb, :, pl.ds(k_start, tk)], p_ref[hb])

    def kv_block(j, masked):
        add_values(jnp.maximum(j - 1, 0))
        k_start = pl.multiple_of(j * tk, tk)
        for hb in range(2):
            s = lax.dot_general(k_ref[0, hb, pl.ds(k_start, tk), :], q_ref[0, hb], (((1,), (1,)), ((), ())),
                                preferred_element_type=F32)
            if masked:
                k_pos = k_start + lax.broadcasted_iota(jnp.int32, (tk, tq), 0)
                q_pos = q_first + lax.broadcasted_iota(jnp.int32, (tk, tq), 1)
                s = jnp.where(k_pos <= q_pos, s, -jnp.inf)
            m_prev = m_ref[hb]
            m_new = jnp.maximum(m_prev, jnp.max(s, axis=0, keepdims=True))
            alpha_ref[hb] = jnp.exp2(m_prev - m_new)
            p_ref[hb] = jnp.exp2(s - m_new).astype(BF16)
            m_ref[hb] = m_new

    def full_block(j, carry):
        kv_block(j, False)
        return carry

    lax.fori_loop(0, n_full, full_block, 0)
    kv_block(n_full, True)
    add_values(n_full)
    heads = [acc_ref[hb, 0:HEAD_DIM, :] / acc_ref[hb, HEAD_DIM:HEAD_DIM + 1, :] for hb in range(2)]
    o_ref[0] = jnp.concatenate(heads, axis=0).T.astype(o_ref.dtype)


def _past_keys_aug(k_past, lc_past):
    b, p = k_past.shape[:2]
    lc_past = lc_past * LOG2E
    hi = lc_past.astype(BF16)
    r1 = lc_past - hi.astype(F32)
    mid = r1.astype(BF16)
    lo = (r1 - mid.astype(F32)).astype(BF16)
    ones = jnp.ones_like(hi)
    bias = jnp.stack([ones, ones, ones, -hi, -mid, -lo], axis=-1)
    feats = jnp.transpose(k_past.astype(BF16), (0, 2, 1, 3))
    pad = jnp.zeros((b, N_HEADS, p, LANES - HEAD_DIM - N_BIAS), BF16)
    return jnp.concatenate([feats, bias, pad], axis=-1)


def _value_rows(v_t):
    b, h, _, t = v_t.shape
    return jnp.concatenate([v_t, jnp.ones((b, h, 1, t), v_t.dtype),
                            jnp.zeros((b, h, V_ROWS - HEAD_DIM - 1, t), v_t.dtype)], axis=2)


def _fox_attention(q_aug, k_aug, v_t, past, tq, tk):
    b, _, t_q, _ = q_aug.shape
    t_k = k_aug.shape[2]
    assert t_q % tq == 0 and t_k % tk == 0 and tk % tq == 0 and past % tq == 0
    return pl.pallas_call(
        functools.partial(_attn_kernel, tq=tq, tk=tk, past=past),
        out_shape=jax.ShapeDtypeStruct((b, t_q, WIDTH), BF16),
        grid=(b, N_PAIRS, t_q // tq),
        in_specs=[pl.BlockSpec((1, 2, tq, LANES), lambda i, p, qi: (i, p, qi, 0)),
                  pl.BlockSpec((1, 2, t_k, LANES), lambda i, p, qi: (i, p, 0, 0)),
                  pl.BlockSpec((1, 2, V_ROWS, t_k), lambda i, p, qi: (i, p, 0, 0))],
        out_specs=pl.BlockSpec((1, tq, LANES), lambda i, p, qi: (i, qi, p)),
        scratch_shapes=[pltpu.VMEM((2, 1, tq), F32), pltpu.VMEM((2, 1, tq), F32),
                        pltpu.VMEM((2, tk, tq), BF16), pltpu.VMEM((2, V_ROWS, tq), F32)],
        compiler_params=_cparams(("arbitrary", "arbitrary", "arbitrary")),
        name="fox_attention",
    )(q_aug, k_aug, v_t)


def _out_kernel(x_ref, ys_ref, g_ref, bonus_ref, yf_ref, mod_ref, gng_ref, gnb_ref, bd_ref, wo_ref, n2_ref,
                wg_ref, wu_ref, wd_ref, o_ref):
    bd = bd_ref[...]
    gt1 = mod_ref[0, 2:3, :]
    sh2 = mod_ref[0, 3:4, :]
    sc2 = mod_ref[0, 4:5, :]
    gt2 = mod_ref[0, 5:6, :]
    inv_hd = 1.0 / HEAD_DIM
    ys = ys_ref[0]
    d = ys - _split2_dot(ys, bd) * inv_hd
    var = _split2_dot(d * d, bd) * inv_hd
    yn = d * lax.rsqrt(var + GN_EPS) * gng_ref[...] + gnb_ref[...]
    y_rwkv = ((yn + bonus_ref[0]) * g_ref[0]).astype(BF16)
    mix = _dot(y_rwkv, wo_ref[0:WIDTH, :]) + _dot(yf_ref[0], wo_ref[WIDTH:2 * WIDTH, :])
    x1 = x_ref[0] + gt1 * mix
    h2 = x1 * lax.rsqrt(jnp.mean(x1 * x1, axis=-1, keepdims=True) + NORM_EPS) * n2_ref[...]
    hb = (h2 * (1.0 + sc2) + sh2).astype(BF16)
    act = (_silu(_dot(hb, wg_ref[...])) * _dot(hb, wu_ref[...])).astype(BF16)
    o_ref[0] = x1 + gt2 * _dot(act, wd_ref[...])


def _outproj_ffn(x, ys, g, bonus, yf, modp, wts, tm):
    b, t, _ = x.shape
    tile = lambda w: pl.BlockSpec((1, tm, w), lambda i, j: (i, j, 0))
    consts = [wts[n] for n in ("gn_g", "gn_b", "bd", "w_out", "norm2_g", "w_ffn_gate", "w_ffn_up", "w_ffn_down")]
    return pl.pallas_call(
        _out_kernel,
        out_shape=jax.ShapeDtypeStruct((b, t, D_MODEL), F32),
        grid=(b, t // tm),
        in_specs=([tile(D_MODEL), tile(WIDTH), tile(WIDTH), tile(WIDTH), tile(WIDTH),
                   pl.BlockSpec((1, 6, D_MODEL), lambda i, j: (i, 0, 0))]
                  + [_const_spec(c.shape) for c in consts]),
        out_specs=tile(D_MODEL),
        compiler_params=_cparams(("arbitrary", "arbitrary")),
        name="outproj_ffn",
    )(x, ys, g, bonus, yf, modp, *consts)


def _prep_weights(norm1_g, w_in, shift_mu, w0, w_decay_up, a0, w_aaa_up, w_gate_up, k_k, k_a, r_k, gn_g, gn_b,
                  fox_q_g, fox_k_g, fox_f_b, w_out, norm2_g, w_ffn_gate, w_ffn_up, w_ffn_down):
    row = lambda a: a.reshape(1, -1).astype(F32)
    fox0 = RWKV_COLS
    zeros_lora = jnp.zeros((DECAY_LORA, WIDTH), BF16)
    head = jnp.arange(WIDTH) // HEAD_DIM
    return {
        "norm1_g": row(norm1_g),
        "w_r": w_in[:, :RWKV_COLS].astype(BF16),
        "w_f": w_in[:, fox0:fox0 + 3 * WIDTH].astype(BF16),
        "w_fl": jnp.pad(w_in[:, fox0 + 3 * WIDTH:], ((0, 0), (0, LANES - N_HEADS))).astype(BF16),
        "shift_mu": row(shift_mu), "w0": row(w0), "a0": row(a0),
        "w_du": jnp.concatenate([w_decay_up.astype(BF16), zeros_lora], axis=0),
        "w_au": jnp.concatenate([zeros_lora, w_aaa_up.astype(BF16)], axis=0),
        "w_gu": w_gate_up.astype(BF16),
        "k_k": row(k_k), "k_a": row(k_a), "r_k": row(r_k),
        "fox_q_g": row(jnp.tile(fox_q_g, N_HEADS)), "fox_k_g": row(jnp.tile(fox_k_g, N_HEADS)),
        "fox_f_b": jnp.pad(row(fox_f_b), ((0, 0), (0, LANES - N_HEADS))),
        "bd": (head[:256, None] == head[None, :256]).astype(BF16),
        "gn_g": row(gn_g), "gn_b": row(gn_b),
        "w_out": w_out.astype(BF16), "norm2_g": row(norm2_g),
        "w_ffn_gate": w_ffn_gate.astype(BF16), "w_ffn_up": w_ffn_up.astype(BF16),
        "w_ffn_down": w_ffn_down.astype(BF16),
    }


def _to_pairs(state):
    b = state.shape[0]
    s = state.reshape(b, N_PAIRS, 2, HEAD_DIM, HEAD_DIM)
    return jnp.transpose(s, (0, 1, 4, 2, 3)).reshape(b, N_PAIRS, HEAD_DIM, LANES)


def _from_pairs(state):
    b = state.shape[0]
    s = state.reshape(b, N_PAIRS, HEAD_DIM, 2, HEAD_DIM)
    return jnp.transpose(s, (0, 1, 3, 4, 2)).reshape(b, N_HEADS, HEAD_DIM, HEAD_DIM)


def _round_up(n, m):
    return -(-n // m) * m


def _layer(x, modp, shift_prev, s_prev, k_past, v_past, logf_past, wts, tm, tq):
    b, t, _ = x.shape
    past = k_past.shape[1]
    if past:
        lc_past = _cumsum(jnp.transpose(logf_past.astype(F32), (0, 2, 1)))
        lc_in = jnp.pad(lc_past[:, None, :, past - 1], ((0, 0), (0, 0), (0, LANES - N_HEADS)))
    else:
        lc_in = jnp.zeros((b, 1, LANES), F32)
    (r, w, k2, v, kk, bvec, g, bonus, q_aug, k_aug, v_t, kf, vf, lf_pad, new_shift) = _inproj(
        x, modp, shift_prev, lc_in, wts, tm)

    ys, s_new = _rwkv_scan(r, w, k2, v, kk, bvec, _to_pairs(s_prev.astype(F32)))

    logf = lf_pad[:, :, :N_HEADS]
    tq = min(tq, t)
    if past:
        tk_all = _round_up(past + t, LANES)
        tail = tk_all - past - t
        k_aug = jnp.concatenate([_past_keys_aug(k_past, lc_past), k_aug,
                                 jnp.zeros((b, N_HEADS, tail, LANES), BF16)], axis=2)
        v_t = jnp.concatenate([_value_rows(jnp.transpose(v_past.astype(BF16), (0, 2, 3, 1))), v_t,
                               jnp.zeros((b, N_HEADS, V_ROWS, tail), BF16)], axis=3)
        tk = tk_all if t == tq else tq
    else:
        tk = tq
    yf = _fox_attention(q_aug, k_aug, v_t, past, tq, tk)

    y = _outproj_ffn(x, ys, g, bonus, yf, modp, wts, tm)
    return (y, _from_pairs(s_new), new_shift, kf.reshape(b, t, N_HEADS, HEAD_DIM),
            vf.reshape(b, t, N_HEADS, HEAD_DIM), logf)


def kernel(x_prompt, x_sample, cache_fox_k, cache_fox_v, cache_fox_logf, state_rwkv, state_rwkv_shift, c_prompt, c_sample, norm1_g, w_ada, b_ada, w_in, shift_mu, w0, w_decay_up, a0, w_aaa_up, w_gate_up, k_k, k_a, r_k, gn_g, gn_b, fox_q_g, fox_k_g, fox_f_b, w_out, norm2_g, w_ffn_gate, w_ffn_up, w_ffn_down):
    depth = w_in.shape[0]
    bp, bs = x_prompt.shape[0], x_sample.shape[0]
    zero_shift = jnp.zeros((bp, 1, RWKV_COLS), F32)
    zero_state = jnp.zeros((bp, N_HEADS, HEAD_DIM, HEAD_DIM), F32)
    zero_kv = jnp.zeros((bp, 0, N_HEADS, HEAD_DIM), F32)
    zero_logf = jnp.zeros((bp, 0, N_HEADS), F32)
    hp, hs = x_prompt, x_sample
    outs_p, outs_s = [], []
    for l in range(depth):
        wts = _prep_weights(norm1_g[l], w_in[l], shift_mu[l], w0[l], w_decay_up[l], a0[l], w_aaa_up[l],
                            w_gate_up[l], k_k[l], k_a[l], r_k[l], gn_g[l], gn_b[l], fox_q_g[l], fox_k_g[l],
                            fox_f_b[l], w_out[l], norm2_g[l], w_ffn_gate[l], w_ffn_up[l], w_ffn_down[l])
        rows = _round_up(bp + bs, 8)
        c_all = jnp.pad(jnp.concatenate([c_prompt, c_sample], axis=0), ((0, rows - bp - bs), (0, 0)))
        mod = _adaln(c_all, w_ada[l].astype(BF16), b_ada[l].reshape(1, -1)).reshape(rows, 6, D_MODEL)
        res_p = _layer(hp, mod[:bp], zero_shift, zero_state, zero_kv, zero_kv, zero_logf, wts, tm=256, tq=512)
        res_s = _layer(hs, mod[bp:bp + bs], state_rwkv_shift[l], state_rwkv[l], cache_fox_k[l], cache_fox_v[l],
                       cache_fox_logf[l], wts, tm=64, tq=512)
        hp, hs = res_p[0], res_s[0]
        outs_p.append(res_p[1:])
        outs_s.append(res_s[1:])
    stack = lambda outs, i: jnp.stack([o[i] for o in outs])
    return (hp, hs,
            stack(outs_p, 0), stack(outs_p, 1), stack(outs_p, 2), stack(outs_p, 3), stack(outs_p, 4),
            stack(outs_s, 0), stack(outs_s, 1), stack(outs_s, 2), stack(outs_s, 3), stack(outs_s, 4))
```

```python
import functools

import jax
import jax.numpy as jnp
from jax import lax
from jax.experimental import pallas as pl
from jax.experimental.pallas import tpu as pltpu

F32 = jnp.float32
BF16 = jnp.bfloat16

D_MODEL = 1024
HEAD_DIM = 64
N_HEADS = 8
WIDTH = N_HEADS * HEAD_DIM
N_PAIRS = N_HEADS // 2
DECAY_LORA = 64
AAA_LORA = 64
GATE_LORA = 128
RWKV_COLS = 3 * WIDTH + DECAY_LORA + AAA_LORA + GATE_LORA
LORA_OFF = 3 * WIDTH
D_FF = 2816
NORM_EPS = 1e-6
GN_EPS = 64e-5
LANES = 128
SUBLANES = 8
SCAN_CHUNK = 512
ATTN_KEY_BLOCK = 512
N_BIAS = 6
V_ROWS = 80
LOG2E = 1.4426950408889634
VMEM_LIMIT = 56 * 1024 * 1024


def _cparams(sem):
    return pltpu.CompilerParams(dimension_semantics=sem, vmem_limit_bytes=VMEM_LIMIT)


def _const_spec(shape):
    nd = len(shape)
    return pl.BlockSpec(shape, lambda *_: (0,) * nd, pipeline_mode=pl.Buffered(1))


def _softplus(x):
    return jnp.maximum(x, 0.0) + jnp.log1p(jnp.exp(-jnp.abs(x)))


def _silu(x):
    return x * jax.nn.sigmoid(x)


def _dot(a, b):
    return jnp.dot(a, b, preferred_element_type=F32)


def _split2_dot(x, m):
    hi = x.astype(BF16)
    lo = (x - hi.astype(F32)).astype(BF16)
    half = m.shape[0]
    parts = [_dot(hi[:, c:c + half], m) + _dot(lo[:, c:c + half], m) for c in range(0, x.shape[1], half)]
    return jnp.concatenate(parts, axis=-1)


def _split3_dot_rhs(m, x):
    hi = x.astype(BF16)
    r1 = x - hi.astype(F32)
    mid = r1.astype(BF16)
    lo = (r1 - mid.astype(F32)).astype(BF16)
    return _dot(m, hi) + _dot(m, mid) + _dot(m, lo)


def _mod_kernel(c_ref, w_ref, b_ref, o_ref):
    o_ref[...] = _dot(_silu(c_ref[...]).astype(BF16), w_ref[...]) + b_ref[...]


def _adaln(c_all, w_ada_bf, b_ada):
    rows = c_all.shape[0]
    n = w_ada_bf.shape[1]
    bn = D_MODEL
    return pl.pallas_call(
        _mod_kernel,
        out_shape=jax.ShapeDtypeStruct((rows, n), F32),
        grid=(n // bn,),
        in_specs=[pl.BlockSpec((rows, D_MODEL), lambda j: (0, 0)),
                  pl.BlockSpec((D_MODEL, bn), lambda j: (0, j)),
                  pl.BlockSpec((1, bn), lambda j: (0, j))],
        out_specs=pl.BlockSpec((rows, bn), lambda j: (0, j)),
        compiler_params=_cparams(("arbitrary",)),
        name="adaln_mod",
    )(c_all, w_ada_bf, b_ada)


def _inproj_kernel(x_ref, mod_ref, g1_ref, wr_ref, wf_ref, wfl_ref, sprev_ref, lcin_ref, mu_ref, w0_ref, a0_ref,
                   wdu_ref, wau_ref, wgu_ref, kk_ref, ka_ref, rk_ref, qg_ref, kg_ref, fb_ref, bd_ref,
                   tri_ref, pq_ref, pk_ref, oneq_ref, onek_ref,
                   r_o, w_o, k_o, v_o, kk_o, b_o, g_o, bonus_o, qa_o, ka_o, vt_o, kf_o, vf_o, lf_o, sh_o,
                   carry_ref, lc_ref, *, tm):
    t = pl.program_id(1)
    bd = bd_ref[...]
    x = x_ref[0]
    sh1 = mod_ref[0, 0:1, :]
    sc1 = mod_ref[0, 1:2, :]
    h = x * lax.rsqrt(jnp.mean(x * x, axis=-1, keepdims=True) + NORM_EPS) * g1_ref[...]
    hb = (h * (1.0 + sc1) + sh1).astype(BF16)
    pr = _dot(hb, wr_ref[...])
    pf = _dot(hb, wf_ref[...])
    pfl = _dot(hb, wfl_ref[...])

    @pl.when(t == 0)
    def _():
        carry_ref[...] = sprev_ref[0]

    prev = pltpu.roll(pr, 1, axis=0)
    first_row = lax.broadcasted_iota(jnp.int32, pr.shape, 0) == 0
    prev = jnp.where(first_row, carry_ref[...], prev)
    last = pr[tm - 1:tm, :]
    carry_ref[...] = last
    sh_o[0] = last
    z = pr + (prev - pr) * mu_ref[...]

    r = z[:, 0:WIDTH]
    k = z[:, WIDTH:2 * WIDTH]
    v = z[:, 2 * WIDTH:3 * WIDTH]
    dwa = z[:, LORA_OFF:LORA_OFF + LANES]
    dg = z[:, LORA_OFF + LANES:RWKV_COLS]
    w_log = -_softplus(-(w0_ref[...] + _dot(jnp.tanh(dwa).astype(BF16), wdu_ref[...]))) - 0.5
    decay = jnp.exp(-jnp.exp(w_log))
    a = jax.nn.sigmoid(a0_ref[...] + _dot(dwa.astype(BF16), wau_ref[...]))
    g = _dot(jax.nn.sigmoid(dg).astype(BF16), wgu_ref[...])
    kk = k * kk_ref[...]
    kk = kk / jnp.maximum(jnp.sqrt(_split2_dot(kk * kk, bd)), 1e-12)
    k2 = k * (1.0 + (a - 1.0) * ka_ref[...])
    bonus = _split2_dot(r * k2 * rk_ref[...], bd) * v

    r_o[0] = r
    w_o[0] = decay
    k_o[0] = k2
    v_o[0] = v
    kk_o[0] = kk
    b_o[0] = kk * a
    g_o[0] = g
    bonus_o[0] = bonus

    q = pf[:, 0:WIDTH]
    kf = pf[:, WIDTH:2 * WIDTH]
    vf = pf[:, 2 * WIDTH:3 * WIDTH]
    inv_hd = 1.0 / HEAD_DIM
    qn = q * lax.rsqrt(_split2_dot(q * q, bd) * inv_hd + NORM_EPS) * qg_ref[...]
    kn = kf * lax.rsqrt(_split2_dot(kf * kf, bd) * inv_hd + NORM_EPS) * kg_ref[...]
    kf_o[0] = kn
    vf_o[0] = vf
    logf = -_softplus(-(pfl + fb_ref[...]))
    lf_o[0] = logf

    @pl.when(t == 0)
    def _():
        lc_ref[...] = lcin_ref[0]

    lc = _running_log_forget(logf, tri_ref, lc_ref)
    q_bf = (qn * (HEAD_DIM ** -0.5 * LOG2E)).astype(BF16)
    _attention_operands(q_bf, None, lc, pq_ref, oneq_ref, qa_o, None, tm)
    _attention_operands(kn.astype(BF16), vf, lc, pk_ref, onek_ref, ka_o, vt_o, tm)


def _placement(sign_first):
    rows = jnp.arange(WIDTH + 3 * LANES)[:, None]
    cols = jnp.arange(N_HEADS * LANES)[None, :]
    c_head, c_lane = cols // LANES, cols % LANES
    feat = (rows < WIDTH) & (rows // HEAD_DIM == c_head) & (rows % HEAD_DIM == c_lane)
    piece, p_head = (rows - WIDTH) // LANES, (rows - WIDTH) % LANES
    lc_lane0 = HEAD_DIM if sign_first else HEAD_DIM + 3
    bias = (rows >= WIDTH) & (p_head == c_head) & (c_lane == lc_lane0 + piece)
    place = feat.astype(F32) + bias.astype(F32) * (1.0 if sign_first else -1.0)
    one_lane0 = HEAD_DIM + 3 if sign_first else HEAD_DIM
    ones = ((c_lane >= one_lane0) & (c_lane < one_lane0 + 3)).astype(F32)
    return place.astype(BF16), ones


def _inproj(x, modp, shift_prev, lc_in, wts, tm):
    b, t, _ = x.shape
    nt = t // tm
    tok = lambda w, dt: jax.ShapeDtypeStruct((b, t, w), dt)
    tile = lambda w: pl.BlockSpec((1, tm, w), lambda i, j: (i, j, 0))
    consts = [wts[n] for n in ("norm1_g", "w_r", "w_f", "w_fl")]
    consts2 = [wts[n] for n in ("shift_mu", "w0", "a0", "w_du", "w_au", "w_gu", "k_k", "k_a", "r_k",
                                "fox_q_g", "fox_k_g", "fox_f_b", "bd")]
    tri = (jnp.arange(tm)[:, None] >= jnp.arange(tm)[None, :]).astype(BF16)
    place_q, ones_q = _placement(True)
    place_k, ones_k = _placement(False)
    consts3 = [tri, place_q, place_k, ones_q, ones_k]
    per_stream = lambda w: pl.BlockSpec((1, 1, w), lambda i, j: (i, 0, 0))
    in_specs = ([tile(D_MODEL), pl.BlockSpec((1, 6, D_MODEL), lambda i, j: (i, 0, 0))]
                + [_const_spec(c.shape) for c in consts]
                + [per_stream(RWKV_COLS), per_stream(LANES)]
                + [_const_spec(c.shape) for c in consts2 + consts3])
    head_major = jax.ShapeDtypeStruct((b, N_HEADS, t, LANES), BF16)
    head_tile = pl.BlockSpec((1, N_HEADS, tm, LANES), lambda i, j: (i, 0, j, 0))
    out_shape = ([tok(WIDTH, F32)] * 8
                 + [head_major, head_major, jax.ShapeDtypeStruct((b, N_HEADS, V_ROWS, t), BF16)]
                 + [tok(WIDTH, F32)] * 2 + [tok(LANES, F32), jax.ShapeDtypeStruct((b, 1, RWKV_COLS), F32)])
    out_specs = ([tile(WIDTH)] * 8
                 + [head_tile, head_tile, pl.BlockSpec((1, N_HEADS, V_ROWS, tm), lambda i, j: (i, 0, 0, j))]
                 + [tile(WIDTH)] * 2 + [tile(LANES), per_stream(RWKV_COLS)])
    return pl.pallas_call(
        functools.partial(_inproj_kernel, tm=tm),
        out_shape=out_shape,
        grid=(b, nt),
        in_specs=in_specs,
        out_specs=out_specs,
        scratch_shapes=[pltpu.VMEM((1, RWKV_COLS), F32), pltpu.VMEM((1, LANES), F32)],
        compiler_params=_cparams(("arbitrary", "arbitrary")),
        name="inproj",
    )(x, modp, *consts, shift_prev, lc_in, *consts2, *consts3)


def _exact_pieces(x):
    hi = x.astype(BF16).astype(F32)
    r1 = x - hi
    mid = r1.astype(BF16).astype(F32)
    return hi, mid, r1 - mid


def _scan_kernel(r_ref, w_ref, k_ref, v_ref, kk_ref, b_ref, s0_ref, sel_ref, e_ref, y_ref, s_ref,
                 lhs_ref, tile_ref, gam_ref, *, n_steps):
    t = pl.program_id(1)

    @pl.when(t == 0)
    def _():
        s_ref[...] = s0_ref[...]

    sel = sel_ref[...]
    e_all = e_ref[...]
    row_id = lax.broadcasted_iota(jnp.int32, (SUBLANES, LANES), 0)
    lo8 = lax.broadcasted_iota(jnp.int32, (SUBLANES, LANES), 1) < HEAD_DIM
    n_vec = 5
    vec_cols = 6 * SUBLANES
    q_pad = jnp.zeros((2 * LANES - n_vec * vec_cols, LANES), F32)
    col_vec = lax.broadcasted_iota(jnp.int32, (1, 2 * LANES), 1) // vec_cols

    def transposed_halves(vectors):
        blocks = []
        for x8 in vectors:
            for h in range(2):
                blocks.extend(_exact_pieces(jnp.where(lo8 if h == 0 else jnp.logical_not(lo8), x8, 0.0)))
        q = jnp.concatenate(blocks + [q_pad], axis=0).astype(BF16)
        return lax.dot_general(sel, q, (((1,), (1,)), ((), ())), preferred_element_type=F32).astype(BF16)

    def shift_rows(x, sh, fill):
        return jnp.where(row_id >= sh, pltpu.roll(x, sh, axis=0), fill)

    n_groups = n_steps // SUBLANES
    last_lanes = (SUBLANES - 1) * LANES

    tile_rows = N_PAIRS * 4 * HEAD_DIM

    def transpose_group(group, slot):
        base = pl.multiple_of(jnp.minimum(group, n_groups - 1) * SUBLANES, SUBLANES)
        for p in range(N_PAIRS):
            rows8 = lambda ref: ref[0, pl.ds(base, SUBLANES), pl.ds(p * LANES, LANES)]
            g_inc = rows8(w_ref)
            for sh in (1, 2, 4):
                g_inc = g_inc * shift_rows(g_inc, sh, 1.0)
            g_exc = shift_rows(g_inc, 1, 1.0)
            inv = 1.0 / g_inc
            scaled = (rows8(kk_ref) * g_exc, rows8(b_ref) * inv, rows8(k_ref) * inv, rows8(r_ref) * g_inc, g_inc)
            lhs_all = transposed_halves(scaled)
            zero = jnp.zeros((), BF16)
            for vec in range(n_vec):
                row0 = (p * 4 + vec) * HEAD_DIM if vec < 4 else tile_rows + p * HEAD_DIM
                lhs_ref[slot, row0:row0 + HEAD_DIM, :] = jnp.where(col_vec == vec, lhs_all, zero)

    def expand_group(slot_in, slot_out):
        tile_ref[slot_out] = _dot(lhs_ref[slot_in, 0:tile_rows, :], e_all)
        gam_ref[slot_out] = _dot(lhs_ref[slot_in, tile_rows:tile_rows + N_PAIRS * HEAD_DIM, :],
                                 e_all[:, last_lanes:last_lanes + LANES])

    def run_tokens(group, slot, states):
        base = pl.multiple_of(group * SUBLANES, SUBLANES)
        states = list(states)
        v8 = [v_ref[0, pl.ds(base, SUBLANES), pl.ds(p * LANES, LANES)] for p in range(N_PAIRS)]
        ys = [[] for _ in range(N_PAIRS)]
        for u in range(SUBLANES):
            for p in range(N_PAIRS):
                def col(vec):
                    row0 = (p * 4 + vec) * HEAD_DIM
                    return tile_ref[slot, row0:row0 + HEAD_DIM, u * LANES:(u + 1) * LANES]
                s = states[p]
                sa = jnp.sum(s * col(0), axis=0, keepdims=True)
                s = s + col(2) * v8[p][u:u + 1, :] - col(1) * sa
                ys[p].append(jnp.sum(s * col(3), axis=0, keepdims=True))
                states[p] = s
        for p in range(N_PAIRS):
            states[p] = states[p] * gam_ref[slot, p * HEAD_DIM:(p + 1) * HEAD_DIM, :]
            y_ref[0, pl.ds(base, SUBLANES), pl.ds(p * LANES, LANES)] = jnp.concatenate(ys[p], axis=0)
        return tuple(states)

    def body(jj, states):
        g0 = 2 * jj
        states = run_tokens(g0, 0, states)
        expand_group(1, 1)
        transpose_group(g0 + 2, 0)
        states = run_tokens(g0 + 1, 1, states)
        expand_group(0, 0)
        transpose_group(g0 + 3, 1)
        return states

    transpose_group(0, 0)
    expand_group(0, 0)
    transpose_group(1, 1)
    init = tuple(s_ref[0, p] for p in range(N_PAIRS))
    final = lax.fori_loop(0, n_groups // 2, body, init)
    for p in range(N_PAIRS):
        s_ref[0, p] = final[p]


def _rwkv_scan(r, w, k, v, kk, bvec, s0_pairs):
    b, t, _ = r.shape
    chunk = min(SCAN_CHUNK, t)
    assert t % chunk == 0 and chunk % SUBLANES == 0
    nt = t // chunk
    lane = jnp.arange(LANES)
    sel = (lane[None, :] % HEAD_DIM == jnp.arange(HEAD_DIM)[:, None]).astype(BF16)
    q_row = jnp.arange(2 * LANES)
    q_half = (q_row % (6 * SUBLANES)) // (3 * SUBLANES)
    q_tok, q_used = q_row % SUBLANES, q_row < 5 * 6 * SUBLANES
    col = jnp.arange(SUBLANES * LANES)
    e_all = (q_used[:, None] & (q_tok[:, None] == col[None, :] // LANES)
             & (q_half[:, None] == (col[None, :] % LANES) // HEAD_DIM)).astype(BF16)
    tile = pl.BlockSpec((1, chunk, WIDTH), lambda i, j: (i, j, 0))
    st = pl.BlockSpec((1, N_PAIRS, HEAD_DIM, LANES), lambda i, j: (i, 0, 0, 0))
    return pl.pallas_call(
        functools.partial(_scan_kernel, n_steps=chunk),
        out_shape=[jax.ShapeDtypeStruct((b, t, WIDTH), F32),
                   jax.ShapeDtypeStruct((b, N_PAIRS, HEAD_DIM, LANES), F32)],
        grid=(b, nt),
        in_specs=[tile] * 6 + [st, _const_spec(sel.shape), _const_spec(e_all.shape)],
        out_specs=[tile, st],
        scratch_shapes=[pltpu.VMEM((2, N_PAIRS * 5 * HEAD_DIM, 2 * LANES), BF16),
                        pltpu.VMEM((2, N_PAIRS * 4 * HEAD_DIM, SUBLANES * LANES), F32),
                        pltpu.VMEM((2, N_PAIRS * HEAD_DIM, LANES), F32)],
        compiler_params=_cparams(("arbitrary", "arbitrary")),
        name="rwkv_scan",
    )(r, w, k, v, kk, bvec, s0_pairs, sel, e_all)


def _attention_operands(feats_bf, v_f32, lc, place_ref, ones_ref, aug_o, vt_o, tm):
    lc = lc * LOG2E
    lc_hi = lc.astype(BF16)
    lc_r1 = lc - lc_hi.astype(F32)
    lc_mid = lc_r1.astype(BF16)
    lc_lo = (lc_r1 - lc_mid.astype(F32)).astype(BF16)
    aug = _dot(jnp.concatenate([feats_bf, lc_hi, lc_mid, lc_lo], axis=-1), place_ref[...]) + ones_ref[...]
    for hd in range(N_HEADS):
        aug_o[0, hd] = aug[:, hd * LANES:(hd + 1) * LANES].astype(BF16)
    if v_f32 is None:
        return
    if tm < LANES:
        v_f32 = jnp.concatenate([v_f32, jnp.zeros((LANES - tm, WIDTH), F32)], axis=0)
    v_t = v_f32.T[:, 0:tm].astype(BF16)
    extra_rows = lax.broadcasted_iota(jnp.int32, (V_ROWS - HEAD_DIM, tm), 0) == 0
    ones_then_zeros = jnp.where(extra_rows, 1.0, 0.0).astype(BF16)
    for hd in range(N_HEADS):
        vt_o[0, hd, 0:HEAD_DIM, :] = v_t[hd * HEAD_DIM:(hd + 1) * HEAD_DIM, :]
        vt_o[0, hd, HEAD_DIM:V_ROWS, :] = ones_then_zeros


def _running_log_forget(logf, tri_ref, lc_ref):
    head_lane = lax.broadcasted_iota(jnp.int32, (1, LANES), 1) < N_HEADS
    lc = _split3_dot_rhs(tri_ref[...], jnp.where(head_lane, logf, 0.0)) + lc_ref[...]
    lc_ref[...] = lc[lc.shape[0] - 1:, :]
    return lc


def _past_kernel(k_ref, v_ref, lf_ref, tri_ref, pk_ref, onek_ref, ka_o, vt_o, lc_o, lc_ref, *, tm):
    @pl.when(pl.program_id(1) == 0)
    def _():
        lc_ref[...] = jnp.zeros_like(lc_ref)

    lc = _running_log_forget(lf_ref[0], tri_ref, lc_ref)
    lc_o[0] = lc_ref[...]
    _attention_operands(k_ref[0].astype(BF16), v_ref[0], lc, pk_ref, onek_ref, ka_o, vt_o, tm)


def _past_cache(k_past, v_past, logf_past):
    b, p = k_past.shape[:2]
    tm = 512
    assert p % tm == 0
    tri = (jnp.arange(tm)[:, None] >= jnp.arange(tm)[None, :]).astype(BF16)
    place_k, ones_k = _placement(False)
    lf = jnp.pad(logf_past.astype(F32), ((0, 0), (0, 0), (0, LANES - N_HEADS)))
    rows = lambda w: pl.BlockSpec((1, tm, w), lambda i, j: (i, j, 0))
    return pl.pallas_call(
        functools.partial(_past_kernel, tm=tm),
        out_shape=[jax.ShapeDtypeStruct((b, N_HEADS, p, LANES), BF16),
                   jax.ShapeDtypeStruct((b, N_HEADS, V_ROWS, p), BF16),
                   jax.ShapeDtypeStruct((b, 1, LANES), F32)],
        grid=(b, p // tm),
        in_specs=[rows(WIDTH), rows(WIDTH), rows(LANES),
                  _const_spec(tri.shape), _const_spec(place_k.shape), _const_spec(ones_k.shape)],
        out_specs=[pl.BlockSpec((1, N_HEADS, tm, LANES), lambda i, j: (i, 0, j, 0)),
                   pl.BlockSpec((1, N_HEADS, V_ROWS, tm), lambda i, j: (i, 0, 0, j)),
                   pl.BlockSpec((1, 1, LANES), lambda i, j: (i, 0, 0))],
        scratch_shapes=[pltpu.VMEM((1, LANES), F32)],
        compiler_params=_cparams(("arbitrary", "arbitrary")),
        name="past_cache",
    )(k_past.reshape(b, p, WIDTH).astype(F32), v_past.reshape(b, p, WIDTH).astype(F32), lf, tri, place_k, ones_k)


def _attn_kernel(q_ref, k_ref, vt_ref, o_ref, m_ref, alpha_ref, s_ref, p_ref, acc_ref, *, tq, tk, past):
    qi = pl.program_id(2)
    q_first = past + qi * tq
    n_full = q_first // tk

    m_ref[...] = jnp.full_like(m_ref, -jnp.inf)
    alpha_ref[...] = jnp.ones_like(alpha_ref)
    p_ref[...] = jnp.zeros_like(p_ref)
    acc_ref[...] = jnp.zeros_like(acc_ref)

    def scores(j, slot):
        k_start = pl.multiple_of(j * tk, tk)
        for hb in range(2):
            s_ref[slot, hb] = lax.dot_general(k_ref[0, hb, pl.ds(k_start, tk), :], q_ref[0, hb],
                                              (((1,), (1,)), ((), ())), preferred_element_type=F32)

    def add_values(j, slot):
        k_start = pl.multiple_of(jnp.maximum(j, 0) * tk, tk)
        for hb in range(2):
            acc_ref[hb] = (alpha_ref[slot, hb] * acc_ref[hb]
                           + _dot(vt_ref[0, hb, :, pl.ds(k_start, tk)], p_ref[slot, hb]))

    def softmax(j, slot, masked):
        for hb in range(2):
            s = s_ref[slot, hb]
            if masked:
                k_pos = j * tk + lax.broadcasted_iota(jnp.int32, (tk, tq), 0)
                q_pos = q_first + lax.broadcasted_iota(jnp.int32, (tk, tq), 1)
                s = jnp.where(k_pos <= q_pos, s, -jnp.inf)
            m_prev = m_ref[hb]
            m_new = jnp.maximum(m_prev, jnp.max(s, axis=0, keepdims=True))
            alpha_ref[slot, hb] = jnp.exp2(m_prev - m_new)
            p_ref[slot, hb] = jnp.exp2(s - m_new).astype(BF16)
            m_ref[hb] = m_new

    def step(j, slot, last):
        add_values(j - 1, 1 - slot)
        softmax(j, slot, masked=last)
        if not last:
            scores(j + 1, 1 - slot)

    def block_pair(jj, carry):
        step(2 * jj, 0, False)
        step(2 * jj + 1, 1, False)
        return carry

    scores(0, 0)
    lax.fori_loop(0, n_full // 2, block_pair, 0)

    @pl.when(n_full % 2 == 1)
    def _():
        step(n_full - 1, 0, False)
        step(n_full, 1, True)
        add_values(n_full, 1)

    @pl.when(n_full % 2 == 0)
    def _():
        step(n_full, 0, True)
        add_values(n_full, 0)

    heads = [acc_ref[hb, 0:HEAD_DIM, :] / acc_ref[hb, HEAD_DIM:HEAD_DIM + 1, :] for hb in range(2)]
    o_ref[0] = jnp.concatenate(heads, axis=0).T.astype(o_ref.dtype)


def _fox_attention(q_aug, k_aug, v_t, past, tq, tk):
    b, _, t_q, _ = q_aug.shape
    t_k = k_aug.shape[2]
    assert t_q % tq == 0 and t_k % tk == 0 and tk % tq == 0 and past % tq == 0
    return pl.pallas_call(
        functools.partial(_attn_kernel, tq=tq, tk=tk, past=past),
        out_shape=jax.ShapeDtypeStruct((b, t_q, WIDTH), BF16),
        grid=(b, N_PAIRS, t_q // tq),
        in_specs=[pl.BlockSpec((1, 2, tq, LANES), lambda i, p, qi: (i, p, qi, 0)),
                  pl.BlockSpec((1, 2, t_k, LANES), lambda i, p, qi: (i, p, 0, 0)),
                  pl.BlockSpec((1, 2, V_ROWS, t_k), lambda i, p, qi: (i, p, 0, 0))],
        out_specs=pl.BlockSpec((1, tq, LANES), lambda i, p, qi: (i, qi, p)),
        scratch_shapes=[pltpu.VMEM((2, 1, tq), F32), pltpu.VMEM((2, 2, 1, tq), F32),
                        pltpu.VMEM((2, 2, tk, tq), F32), pltpu.VMEM((2, 2, tk, tq), BF16),
                        pltpu.VMEM((2, V_ROWS, tq), F32)],
        compiler_params=_cparams(("arbitrary", "arbitrary", "arbitrary")),
        name="fox_attention",
    )(q_aug, k_aug, v_t)


def _out_kernel(x_ref, ys_ref, g_ref, bonus_ref, yf_ref, mod_ref, gng_ref, gnb_ref, bd_ref, wo_ref, n2_ref,
                wg_ref, wu_ref, wd_ref, o_ref):
    bd = bd_ref[...]
    gt1 = mod_ref[0, 2:3, :]
    sh2 = mod_ref[0, 3:4, :]
    sc2 = mod_ref[0, 4:5, :]
    gt2 = mod_ref[0, 5:6, :]
    inv_hd = 1.0 / HEAD_DIM
    ys = ys_ref[0]
    d = ys - _split2_dot(ys, bd) * inv_hd
    var = _split2_dot(d * d, bd) * inv_hd
    yn = d * lax.rsqrt(var + GN_EPS) * gng_ref[...] + gnb_ref[...]
    y_rwkv = ((yn + bonus_ref[0]) * g_ref[0]).astype(BF16)
    mix = _dot(y_rwkv, wo_ref[0:WIDTH, :]) + _dot(yf_ref[0], wo_ref[WIDTH:2 * WIDTH, :])
    x1 = x_ref[0] + gt1 * mix
    h2 = x1 * lax.rsqrt(jnp.mean(x1 * x1, axis=-1, keepdims=True) + NORM_EPS) * n2_ref[...]
    hb = (h2 * (1.0 + sc2) + sh2).astype(BF16)
    act = (_silu(_dot(hb, wg_ref[...])) * _dot(hb, wu_ref[...])).astype(BF16)
    o_ref[0] = x1 + gt2 * _dot(act, wd_ref[...])


def _outproj_ffn(x, ys, g, bonus, yf, modp, wts, tm):
    b, t, _ = x.shape
    tile = lambda w: pl.BlockSpec((1, tm, w), lambda i, j: (i, j, 0))
    consts = [wts[n] for n in ("gn_g", "gn_b", "bd", "w_out", "norm2_g", "w_ffn_gate", "w_ffn_up", "w_ffn_down")]
    return pl.pallas_call(
        _out_kernel,
        out_shape=jax.ShapeDtypeStruct((b, t, D_MODEL), F32),
        grid=(b, t // tm),
        in_specs=([tile(D_MODEL), tile(WIDTH), tile(WIDTH), tile(WIDTH), tile(WIDTH),
                   pl.BlockSpec((1, 6, D_MODEL), lambda i, j: (i, 0, 0))]
                  + [_const_spec(c.shape) for c in consts]),
        out_specs=tile(D_MODEL),
        compiler_params=_cparams(("arbitrary", "arbitrary")),
        name="outproj_ffn",
    )(x, ys, g, bonus, yf, modp, *consts)


def _prep_weights(norm1_g, w_in, shift_mu, w0, w_decay_up, a0, w_aaa_up, w_gate_up, k_k, k_a, r_k, gn_g, gn_b,
                  fox_q_g, fox_k_g, fox_f_b, w_out, norm2_g, w_ffn_gate, w_ffn_up, w_ffn_down):
    row = lambda a: a.reshape(1, -1).astype(F32)
    fox0 = RWKV_COLS
    zeros_lora = jnp.zeros((DECAY_LORA, WIDTH), BF16)
    head = jnp.arange(WIDTH) // HEAD_DIM
    return {
        "norm1_g": row(norm1_g),
        "w_r": w_in[:, :RWKV_COLS].astype(BF16),
        "w_f": w_in[:, fox0:fox0 + 3 * WIDTH].astype(BF16),
        "w_fl": jnp.pad(w_in[:, fox0 + 3 * WIDTH:], ((0, 0), (0, LANES - N_HEADS))).astype(BF16),
        "shift_mu": row(shift_mu), "w0": row(w0), "a0": row(a0),
        "w_du": jnp.concatenate([w_decay_up.astype(BF16), zeros_lora], axis=0),
        "w_au": jnp.concatenate([zeros_lora, w_aaa_up.astype(BF16)], axis=0),
        "w_gu": w_gate_up.astype(BF16),
        "k_k": row(k_k), "k_a": row(k_a), "r_k": row(r_k),
        "fox_q_g": row(jnp.tile(fox_q_g, N_HEADS)), "fox_k_g": row(jnp.tile(fox_k_g, N_HEADS)),
        "fox_f_b": jnp.pad(row(fox_f_b), ((0, 0), (0, LANES - N_HEADS))),
        "bd": (head[:256, None] == head[None, :256]).astype(BF16),
        "gn_g": row(gn_g), "gn_b": row(gn_b),
        "w_out": w_out.astype(BF16), "norm2_g": row(norm2_g),
        "w_ffn_gate": w_ffn_gate.astype(BF16), "w_ffn_up": w_ffn_up.astype(BF16),
        "w_ffn_down": w_ffn_down.astype(BF16),
    }


def _to_pairs(state):
    b = state.shape[0]
    s = state.reshape(b, N_PAIRS, 2, HEAD_DIM, HEAD_DIM)
    return jnp.transpose(s, (0, 1, 4, 2, 3)).reshape(b, N_PAIRS, HEAD_DIM, LANES)


def _from_pairs(state):
    b = state.shape[0]
    s = state.reshape(b, N_PAIRS, HEAD_DIM, 2, HEAD_DIM)
    return jnp.transpose(s, (0, 1, 3, 4, 2)).reshape(b, N_HEADS, HEAD_DIM, HEAD_DIM)


def _round_up(n, m):
    return -(-n // m) * m


def _layer(x, modp, shift_prev, s_prev, k_past, v_past, logf_past, wts, tm, tq):
    b, t, _ = x.shape
    past = k_past.shape[1]
    if past:
        k_aug_past, v_t_past, lc_in = _past_cache(k_past, v_past, logf_past)
    else:
        lc_in = jnp.zeros((b, 1, LANES), F32)
    (r, w, k2, v, kk, bvec, g, bonus, q_aug, k_aug, v_t, kf, vf, lf_pad, new_shift) = _inproj(
        x, modp, shift_prev, lc_in, wts, tm)

    ys, s_new = _rwkv_scan(r, w, k2, v, kk, bvec, _to_pairs(s_prev.astype(F32)))

    logf = lf_pad[:, :, :N_HEADS]
    t_q = _round_up(t, LANES)
    q_aug = jnp.pad(q_aug, ((0, 0), (0, 0), (0, t_q - t), (0, 0)))
    tq = min(tq, t_q)
    if past:
        tk_all = _round_up(past + t_q, LANES)
        tail = tk_all - past - t
        k_aug = jnp.concatenate([k_aug_past, k_aug, jnp.zeros((b, N_HEADS, tail, LANES), BF16)], axis=2)
        v_t = jnp.concatenate([v_t_past, v_t, jnp.zeros((b, N_HEADS, V_ROWS, tail), BF16)], axis=3)
        tk = tk_all if t_q == tq else tq
    else:
        tk = max(tq, ATTN_KEY_BLOCK)
    yf = _fox_attention(q_aug, k_aug, v_t, past, tq, tk)[:, :t]

    y = _outproj_ffn(x, ys, g, bonus, yf, modp, wts, tm)
    return (y, _from_pairs(s_new), new_shift, kf.reshape(b, t, N_HEADS, HEAD_DIM),
            vf.reshape(b, t, N_HEADS, HEAD_DIM), logf)


def kernel(x_prompt, x_sample, cache_fox_k, cache_fox_v, cache_fox_logf, state_rwkv, state_rwkv_shift, c_prompt, c_sample, norm1_g, w_ada, b_ada, w_in, shift_mu, w0, w_decay_up, a0, w_aaa_up, w_gate_up, k_k, k_a, r_k, gn_g, gn_b, fox_q_g, fox_k_g, fox_f_b, w_out, norm2_g, w_ffn_gate, w_ffn_up, w_ffn_down):
    depth = w_in.shape[0]
    bp, bs = x_prompt.shape[0], x_sample.shape[0]
    zero_shift = jnp.zeros((bp, 1, RWKV_COLS), F32)
    zero_state = jnp.zeros((bp, N_HEADS, HEAD_DIM, HEAD_DIM), F32)
    zero_kv = jnp.zeros((bp, 0, N_HEADS, HEAD_DIM), F32)
    zero_logf = jnp.zeros((bp, 0, N_HEADS), F32)
    hp, hs = x_prompt, x_sample
    outs_p, outs_s = [], []
    for l in range(depth):
        wts = _prep_weights(norm1_g[l], w_in[l], shift_mu[l], w0[l], w_decay_up[l], a0[l], w_aaa_up[l],
                            w_gate_up[l], k_k[l], k_a[l], r_k[l], gn_g[l], gn_b[l], fox_q_g[l], fox_k_g[l],
                            fox_f_b[l], w_out[l], norm2_g[l], w_ffn_gate[l], w_ffn_up[l], w_ffn_down[l])
        rows = _round_up(bp + bs, 8)
        c_all = jnp.pad(jnp.concatenate([c_prompt, c_sample], axis=0), ((0, rows - bp - bs), (0, 0)))
        mod = _adaln(c_all, w_ada[l].astype(BF16), b_ada[l].reshape(1, -1)).reshape(rows, 6, D_MODEL)
        res_p = _layer(hp, mod[:bp], zero_shift, zero_state, zero_kv, zero_kv, zero_logf, wts, tm=256, tq=512)
        res_s = _layer(hs, mod[bp:bp + bs], state_rwkv_shift[l], state_rwkv[l], cache_fox_k[l], cache_fox_v[l],
                       cache_fox_logf[l], wts, tm=64, tq=512)
        hp, hs = res_p[0], res_s[0]
        outs_p.append(res_p[1:])
        outs_s.append(res_s[1:])
    stack = lambda outs, i: jnp.stack([o[i] for o in outs])
    return (hp, hs,
            stack(outs_p, 0), stack(outs_p, 1), stack(outs_p, 2), stack(outs_p, 3), stack(outs_p, 4),
            stack(outs_s, 0), stack(outs_s, 1), stack(outs_s, 2), stack(outs_s, 3), stack(outs_s, 4))
```

```python
import functools

import jax
import jax.numpy as jnp
from jax import lax
from jax.experimental import pallas as pl
from jax.experimental.pallas import tpu as pltpu

F32 = jnp.float32
BF16 = jnp.bfloat16

D_MODEL = 1024
HEAD_DIM = 64
N_HEADS = 8
WIDTH = N_HEADS * HEAD_DIM
N_PAIRS = N_HEADS // 2
DECAY_LORA = 64
AAA_LORA = 64
GATE_LORA = 128
RWKV_COLS = 3 * WIDTH + DECAY_LORA + AAA_LORA + GATE_LORA
LORA_OFF = 3 * WIDTH
D_FF = 2816
NORM_EPS = 1e-6
GN_EPS = 64e-5
LANES = 128
SUBLANES = 8
SCAN_CHUNK = 512
ATTN_KEY_BLOCK = 512
N_BIAS = 6
V_ROWS = 80
LOG2E = 1.4426950408889634
VMEM_LIMIT = 56 * 1024 * 1024


def _cparams(sem):
    return pltpu.CompilerParams(dimension_semantics=sem, vmem_limit_bytes=VMEM_LIMIT)


def _const_spec(shape):
    nd = len(shape)
    return pl.BlockSpec(shape, lambda *_: (0,) * nd, pipeline_mode=pl.Buffered(1))


def _softplus(x):
    return jnp.maximum(x, 0.0) + jnp.log1p(jnp.exp(-jnp.abs(x)))


def _silu(x):
    return x * jax.nn.sigmoid(x)


def _dot(a, b):
    return jnp.dot(a, b, preferred_element_type=F32)


def _split2_dot(x, m):
    hi = x.astype(BF16)
    lo = (x - hi.astype(F32)).astype(BF16)
    half = m.shape[0]
    parts = [_dot(hi[:, c:c + half], m) + _dot(lo[:, c:c + half], m) for c in range(0, x.shape[1], half)]
    return jnp.concatenate(parts, axis=-1)


def _split3_dot_rhs(m, x):
    hi = x.astype(BF16)
    r1 = x - hi.astype(F32)
    mid = r1.astype(BF16)
    lo = (r1 - mid.astype(F32)).astype(BF16)
    return _dot(m, hi) + _dot(m, mid) + _dot(m, lo)


def _mod_kernel(c_ref, w_ref, b_ref, o_ref):
    o_ref[...] = _dot(_silu(c_ref[...]).astype(BF16), w_ref[...]) + b_ref[...]


def _adaln(c_all, w_ada_bf, b_ada):
    rows = c_all.shape[0]
    n = w_ada_bf.shape[1]
    bn = D_MODEL
    return pl.pallas_call(
        _mod_kernel,
        out_shape=jax.ShapeDtypeStruct((rows, n), F32),
        grid=(n // bn,),
        in_specs=[pl.BlockSpec((rows, D_MODEL), lambda j: (0, 0)),
                  pl.BlockSpec((D_MODEL, bn), lambda j: (0, j)),
                  pl.BlockSpec((1, bn), lambda j: (0, j))],
        out_specs=pl.BlockSpec((rows, bn), lambda j: (0, j)),
        compiler_params=_cparams(("arbitrary",)),
        name="adaln_mod",
    )(c_all, w_ada_bf, b_ada)


def _inproj_kernel(x_ref, mod_ref, g1_ref, wr_ref, wf_ref, wfl_ref, sprev_ref, lcin_ref, mu_ref, w0_ref, a0_ref,
                   wdu_ref, wau_ref, wgu_ref, kk_ref, ka_ref, rk_ref, qg_ref, kg_ref, fb_ref, bd_ref,
                   tri_ref, pq_ref, pk_ref, oneq_ref, onek_ref,
                   r_o, w_o, k_o, v_o, kk_o, b_o, g_o, bonus_o, qa_o, ka_o, vt_o, kf_o, vf_o, lf_o, sh_o,
                   carry_ref, lc_ref, *, tm):
    t = pl.program_id(1)
    bd = bd_ref[...]
    x = x_ref[0]
    sh1 = mod_ref[0, 0:1, :]
    sc1 = mod_ref[0, 1:2, :]
    h = x * lax.rsqrt(jnp.mean(x * x, axis=-1, keepdims=True) + NORM_EPS) * g1_ref[...]
    hb = (h * (1.0 + sc1) + sh1).astype(BF16)
    pr = _dot(hb, wr_ref[...])
    pf = _dot(hb, wf_ref[...])
    pfl = _dot(hb, wfl_ref[...])

    @pl.when(t == 0)
    def _():
        carry_ref[...] = sprev_ref[0]

    prev = pltpu.roll(pr, 1, axis=0)
    first_row = lax.broadcasted_iota(jnp.int32, pr.shape, 0) == 0
    prev = jnp.where(first_row, carry_ref[...], prev)
    last = pr[tm - 1:tm, :]
    carry_ref[...] = last
    sh_o[0] = last
    z = pr + (prev - pr) * mu_ref[...]

    r = z[:, 0:WIDTH]
    k = z[:, WIDTH:2 * WIDTH]
    v = z[:, 2 * WIDTH:3 * WIDTH]
    dwa = z[:, LORA_OFF:LORA_OFF + LANES]
    dg = z[:, LORA_OFF + LANES:RWKV_COLS]
    w_log = -_softplus(-(w0_ref[...] + _dot(jnp.tanh(dwa).astype(BF16), wdu_ref[...]))) - 0.5
    decay = jnp.exp(-jnp.exp(w_log))
    a = jax.nn.sigmoid(a0_ref[...] + _dot(dwa.astype(BF16), wau_ref[...]))
    g = _dot(jax.nn.sigmoid(dg).astype(BF16), wgu_ref[...])
    kk = k * kk_ref[...]
    kk = kk / jnp.maximum(jnp.sqrt(_split2_dot(kk * kk, bd)), 1e-12)
    k2 = k * (1.0 + (a - 1.0) * ka_ref[...])
    bonus = _split2_dot(r * k2 * rk_ref[...], bd) * v

    r_o[0] = r
    w_o[0] = decay
    k_o[0] = k2
    v_o[0] = v
    kk_o[0] = kk
    b_o[0] = kk * a
    g_o[0] = g
    bonus_o[0] = bonus

    q = pf[:, 0:WIDTH]
    kf = pf[:, WIDTH:2 * WIDTH]
    vf = pf[:, 2 * WIDTH:3 * WIDTH]
    inv_hd = 1.0 / HEAD_DIM
    qn = q * lax.rsqrt(_split2_dot(q * q, bd) * inv_hd + NORM_EPS) * qg_ref[...]
    kn = kf * lax.rsqrt(_split2_dot(kf * kf, bd) * inv_hd + NORM_EPS) * kg_ref[...]
    kf_o[0] = kn
    vf_o[0] = vf
    logf = -_softplus(-(pfl + fb_ref[...]))
    lf_o[0] = logf

    @pl.when(t == 0)
    def _():
        lc_ref[...] = lcin_ref[0]

    lc = _running_log_forget(logf, tri_ref, lc_ref)
    q_bf = (qn * (HEAD_DIM ** -0.5 * LOG2E)).astype(BF16)
    _attention_operands(q_bf, None, lc, pq_ref, oneq_ref, qa_o, None, tm)
    _attention_operands(kn.astype(BF16), vf, lc, pk_ref, onek_ref, ka_o, vt_o, tm)


def _placement(sign_first):
    rows = jnp.arange(WIDTH + 3 * LANES)[:, None]
    cols = jnp.arange(N_HEADS * LANES)[None, :]
    c_head, c_lane = cols // LANES, cols % LANES
    feat = (rows < WIDTH) & (rows // HEAD_DIM == c_head) & (rows % HEAD_DIM == c_lane)
    piece, p_head = (rows - WIDTH) // LANES, (rows - WIDTH) % LANES
    lc_lane0 = HEAD_DIM if sign_first else HEAD_DIM + 3
    bias = (rows >= WIDTH) & (p_head == c_head) & (c_lane == lc_lane0 + piece)
    place = feat.astype(F32) + bias.astype(F32) * (1.0 if sign_first else -1.0)
    one_lane0 = HEAD_DIM + 3 if sign_first else HEAD_DIM
    ones = ((c_lane >= one_lane0) & (c_lane < one_lane0 + 3)).astype(F32)
    return place.astype(BF16), ones


def _inproj(x, modp, shift_prev, lc_in, wts, tm):
    b, t, _ = x.shape
    nt = t // tm
    tok = lambda w, dt: jax.ShapeDtypeStruct((b, t, w), dt)
    tile = lambda w: pl.BlockSpec((1, tm, w), lambda i, j: (i, j, 0))
    consts = [wts[n] for n in ("norm1_g", "w_r", "w_f", "w_fl")]
    consts2 = [wts[n] for n in ("shift_mu", "w0", "a0", "w_du", "w_au", "w_gu", "k_k", "k_a", "r_k",
                                "fox_q_g", "fox_k_g", "fox_f_b", "bd")]
    tri = (jnp.arange(tm)[:, None] >= jnp.arange(tm)[None, :]).astype(BF16)
    place_q, ones_q = _placement(True)
    place_k, ones_k = _placement(False)
    consts3 = [tri, place_q, place_k, ones_q, ones_k]
    per_stream = lambda w: pl.BlockSpec((1, 1, w), lambda i, j: (i, 0, 0))
    in_specs = ([tile(D_MODEL), pl.BlockSpec((1, 6, D_MODEL), lambda i, j: (i, 0, 0))]
                + [_const_spec(c.shape) for c in consts]
                + [per_stream(RWKV_COLS), per_stream(LANES)]
                + [_const_spec(c.shape) for c in consts2 + consts3])
    head_major = jax.ShapeDtypeStruct((b, N_HEADS, t, LANES), BF16)
    head_tile = pl.BlockSpec((1, N_HEADS, tm, LANES), lambda i, j: (i, 0, j, 0))
    out_shape = ([tok(WIDTH, F32)] * 8
                 + [head_major, head_major, jax.ShapeDtypeStruct((b, N_HEADS, V_ROWS, t), BF16)]
                 + [tok(WIDTH, F32)] * 2 + [tok(LANES, F32), jax.ShapeDtypeStruct((b, 1, RWKV_COLS), F32)])
    out_specs = ([tile(WIDTH)] * 8
                 + [head_tile, head_tile, pl.BlockSpec((1, N_HEADS, V_ROWS, tm), lambda i, j: (i, 0, 0, j))]
                 + [tile(WIDTH)] * 2 + [tile(LANES), per_stream(RWKV_COLS)])
    return pl.pallas_call(
        functools.partial(_inproj_kernel, tm=tm),
        out_shape=out_shape,
        grid=(b, nt),
        in_specs=in_specs,
        out_specs=out_specs,
        scratch_shapes=[pltpu.VMEM((1, RWKV_COLS), F32), pltpu.VMEM((1, LANES), F32)],
        compiler_params=_cparams(("arbitrary", "arbitrary")),
        name="inproj",
    )(x, modp, *consts, shift_prev, lc_in, *consts2, *consts3)


def _exact_pieces(x):
    hi = x.astype(BF16).astype(F32)
    r1 = x - hi
    mid = r1.astype(BF16).astype(F32)
    return hi, mid, r1 - mid


def _scan_kernel(r_ref, w_ref, k_ref, v_ref, kk_ref, b_ref, s0_ref, sel_ref, e_ref, fold_ref, y_ref, s_ref,
                 lhs_ref, tile_ref, rt_ref, gam_ref, *, n_steps):
    t = pl.program_id(1)

    @pl.when(t == 0)
    def _():
        s_ref[...] = s0_ref[...]

    sel = sel_ref[...]
    e_all = e_ref[...]
    row_id = lax.broadcasted_iota(jnp.int32, (SUBLANES, LANES), 0)
    lo8 = lax.broadcasted_iota(jnp.int32, (SUBLANES, LANES), 1) < HEAD_DIM
    n_vec = 5
    vec_cols = 6 * SUBLANES
    q_pad = jnp.zeros((2 * LANES - n_vec * vec_cols, LANES), F32)
    col_vec = lax.broadcasted_iota(jnp.int32, (1, 2 * LANES), 1) // vec_cols

    def transposed_halves(vectors):
        blocks = []
        for x8 in vectors:
            for h in range(2):
                blocks.extend(_exact_pieces(jnp.where(lo8 if h == 0 else jnp.logical_not(lo8), x8, 0.0)))
        q = jnp.concatenate(blocks + [q_pad], axis=0).astype(BF16)
        return lax.dot_general(sel, q, (((1,), (1,)), ((), ())), preferred_element_type=F32).astype(BF16)

    def shift_rows(x, sh, fill):
        return jnp.where(row_id >= sh, pltpu.roll(x, sh, axis=0), fill)

    n_groups = n_steps // SUBLANES
    last_lanes = (SUBLANES - 1) * LANES

    tile_rows = N_PAIRS * 3 * HEAD_DIM

    def transpose_group(group, slot):
        base = pl.multiple_of(jnp.minimum(group, n_groups - 1) * SUBLANES, SUBLANES)
        for p in range(N_PAIRS):
            rows8 = lambda ref: ref[0, pl.ds(base, SUBLANES), pl.ds(p * LANES, LANES)]
            g_inc = rows8(w_ref)
            for sh in (1, 2, 4):
                g_inc = g_inc * shift_rows(g_inc, sh, 1.0)
            g_exc = shift_rows(g_inc, 1, 1.0)
            inv = 1.0 / g_inc
            scaled = (rows8(kk_ref) * g_exc, rows8(b_ref) * inv, rows8(k_ref) * inv, rows8(r_ref) * g_inc, g_inc)
            lhs_all = transposed_halves(scaled)
            zero = jnp.zeros((), BF16)
            for vec in range(n_vec):
                if vec < 3:
                    row0 = (p * 3 + vec) * HEAD_DIM
                else:
                    row0 = tile_rows + ((vec - 3) * N_PAIRS + p) * HEAD_DIM
                lhs_ref[slot, row0:row0 + HEAD_DIM, :] = jnp.where(col_vec == vec, lhs_all, zero)

    pair_rows = N_PAIRS * HEAD_DIM

    def expand_group(slot_in, slot_out):
        tile_ref[slot_out] = _dot(lhs_ref[slot_in, 0:tile_rows, :], e_all)
        rt_ref[slot_out] = _dot(lhs_ref[slot_in, tile_rows:tile_rows + pair_rows, :], fold_ref[...])
        gam_ref[slot_out] = _dot(lhs_ref[slot_in, tile_rows + pair_rows:tile_rows + 2 * pair_rows, :],
                                 e_all[:, last_lanes:last_lanes + LANES])

    lane_half = lax.broadcasted_iota(jnp.int32, (HEAD_DIM, LANES), 1) // HEAD_DIM

    def run_tokens(group, slot, states):
        base = pl.multiple_of(group * SUBLANES, SUBLANES)
        states = list(states)
        v8 = [v_ref[0, pl.ds(base, SUBLANES), pl.ds(p * LANES, LANES)] for p in range(N_PAIRS)]
        ys = [[] for _ in range(N_PAIRS)]
        for u in range(SUBLANES):
            for p in range(N_PAIRS):
                def col(vec):
                    row0 = (p * 3 + vec) * HEAD_DIM
                    return tile_ref[slot, row0:row0 + HEAD_DIM, u * LANES:(u + 1) * LANES]
                s = states[p]
                sa = jnp.sum(s * col(0), axis=0, keepdims=True)
                s = s + col(2) * v8[p][u:u + 1, :] - col(1) * sa
                r_col = jnp.take_along_axis(rt_ref[slot, p * HEAD_DIM:(p + 1) * HEAD_DIM, :],
                                            lane_half * SUBLANES + u, axis=1)
                ys[p].append(jnp.sum(s * r_col, axis=0, keepdims=True))
                states[p] = s
        for p in range(N_PAIRS):
            states[p] = states[p] * gam_ref[slot, p * HEAD_DIM:(p + 1) * HEAD_DIM, :]
            y_ref[0, pl.ds(base, SUBLANES), pl.ds(p * LANES, LANES)] = jnp.concatenate(ys[p], axis=0)
        return tuple(states)

    def body(jj, states):
        g0 = 2 * jj
        states = run_tokens(g0, 0, states)
        expand_group(1, 1)
        transpose_group(g0 + 2, 0)
        states = run_tokens(g0 + 1, 1, states)
        expand_group(0, 0)
        transpose_group(g0 + 3, 1)
        return states

    transpose_group(0, 0)
    expand_group(0, 0)
    transpose_group(1, 1)
    init = tuple(s_ref[0, p] for p in range(N_PAIRS))
    final = lax.fori_loop(0, n_groups // 2, body, init)
    for p in range(N_PAIRS):
        s_ref[0, p] = final[p]


def _rwkv_scan(r, w, k, v, kk, bvec, s0_pairs):
    b, t, _ = r.shape
    chunk = min(SCAN_CHUNK, t)
    assert t % chunk == 0 and chunk % SUBLANES == 0
    nt = t // chunk
    lane = jnp.arange(LANES)
    sel = (lane[None, :] % HEAD_DIM == jnp.arange(HEAD_DIM)[:, None]).astype(BF16)
    q_row = jnp.arange(2 * LANES)
    q_half = (q_row % (6 * SUBLANES)) // (3 * SUBLANES)
    q_tok, q_used = q_row % SUBLANES, q_row < 5 * 6 * SUBLANES
    col = jnp.arange(SUBLANES * LANES)
    e_all = (q_used[:, None] & (q_tok[:, None] == col[None, :] // LANES)
             & (q_half[:, None] == (col[None, :] % LANES) // HEAD_DIM)).astype(BF16)
    fold = (q_used[:, None] & (lane[None, :] == q_half[:, None] * SUBLANES + q_tok[:, None])).astype(BF16)
    tile = pl.BlockSpec((1, chunk, WIDTH), lambda i, j: (i, j, 0))
    st = pl.BlockSpec((1, N_PAIRS, HEAD_DIM, LANES), lambda i, j: (i, 0, 0, 0))
    return pl.pallas_call(
        functools.partial(_scan_kernel, n_steps=chunk),
        out_shape=[jax.ShapeDtypeStruct((b, t, WIDTH), F32),
                   jax.ShapeDtypeStruct((b, N_PAIRS, HEAD_DIM, LANES), F32)],
        grid=(b, nt),
        in_specs=[tile] * 6 + [st, _const_spec(sel.shape), _const_spec(e_all.shape), _const_spec(fold.shape)],
        out_specs=[tile, st],
        scratch_shapes=[pltpu.VMEM((2, N_PAIRS * 5 * HEAD_DIM, 2 * LANES), BF16),
                        pltpu.VMEM((2, N_PAIRS * 3 * HEAD_DIM, SUBLANES * LANES), F32),
                        pltpu.VMEM((2, N_PAIRS * HEAD_DIM, LANES), F32),
                        pltpu.VMEM((2, N_PAIRS * HEAD_DIM, LANES), F32)],
        compiler_params=_cparams(("arbitrary", "arbitrary")),
        name="rwkv_scan",
    )(r, w, k, v, kk, bvec, s0_pairs, sel, e_all, fold)


def _attention_operands(feats_bf, v_f32, lc, place_ref, ones_ref, aug_o, vt_o, tm):
    lc = lc * LOG2E
    lc_hi = lc.astype(BF16)
    lc_r1 = lc - lc_hi.astype(F32)
    lc_mid = lc_r1.astype(BF16)
    lc_lo = (lc_r1 - lc_mid.astype(F32)).astype(BF16)
    aug = _dot(jnp.concatenate([feats_bf, lc_hi, lc_mid, lc_lo], axis=-1), place_ref[...]) + ones_ref[...]
    for hd in range(N_HEADS):
        aug_o[0, hd] = aug[:, hd * LANES:(hd + 1) * LANES].astype(BF16)
    if v_f32 is None:
        return
    if tm < LANES:
        v_f32 = jnp.concatenate([v_f32, jnp.zeros((LANES - tm, WIDTH), F32)], axis=0)
    v_t = v_f32.T[:, 0:tm].astype(BF16)
    extra_rows = lax.broadcasted_iota(jnp.int32, (V_ROWS - HEAD_DIM, tm), 0) == 0
    ones_then_zeros = jnp.where(extra_rows, 1.0, 0.0).astype(BF16)
    for hd in range(N_HEADS):
        vt_o[0, hd, 0:HEAD_DIM, :] = v_t[hd * HEAD_DIM:(hd + 1) * HEAD_DIM, :]
        vt_o[0, hd, HEAD_DIM:V_ROWS, :] = ones_then_zeros


def _running_log_forget(logf, tri_ref, lc_ref):
    head_lane = lax.broadcasted_iota(jnp.int32, (1, LANES), 1) < N_HEADS
    lc = _split3_dot_rhs(tri_ref[...], jnp.where(head_lane, logf, 0.0)) + lc_ref[...]
    lc_ref[...] = lc[lc.shape[0] - 1:, :]
    return lc


def _past_kernel(k_ref, v_ref, lf_ref, tri_ref, pk_ref, onek_ref, ka_o, vt_o, lc_o, lc_ref, *, tm):
    @pl.when(pl.program_id(1) == 0)
    def _():
        lc_ref[...] = jnp.zeros_like(lc_ref)

    lc = _running_log_forget(lf_ref[0], tri_ref, lc_ref)
    lc_o[0] = lc_ref[...]
    _attention_operands(k_ref[0].astype(BF16), v_ref[0], lc, pk_ref, onek_ref, ka_o, vt_o, tm)


def _past_cache(k_past, v_past, logf_past):
    b, p = k_past.shape[:2]
    tm = 512
    assert p % tm == 0
    tri = (jnp.arange(tm)[:, None] >= jnp.arange(tm)[None, :]).astype(BF16)
    place_k, ones_k = _placement(False)
    lf = jnp.pad(logf_past.astype(F32), ((0, 0), (0, 0), (0, LANES - N_HEADS)))
    rows = lambda w: pl.BlockSpec((1, tm, w), lambda i, j: (i, j, 0))
    return pl.pallas_call(
        functools.partial(_past_kernel, tm=tm),
        out_shape=[jax.ShapeDtypeStruct((b, N_HEADS, p, LANES), BF16),
                   jax.ShapeDtypeStruct((b, N_HEADS, V_ROWS, p), BF16),
                   jax.ShapeDtypeStruct((b, 1, LANES), F32)],
        grid=(b, p // tm),
        in_specs=[rows(WIDTH), rows(WIDTH), rows(LANES),
                  _const_spec(tri.shape), _const_spec(place_k.shape), _const_spec(ones_k.shape)],
        out_specs=[pl.BlockSpec((1, N_HEADS, tm, LANES), lambda i, j: (i, 0, j, 0)),
                   pl.BlockSpec((1, N_HEADS, V_ROWS, tm), lambda i, j: (i, 0, 0, j)),
                   pl.BlockSpec((1, 1, LANES), lambda i, j: (i, 0, 0))],
        scratch_shapes=[pltpu.VMEM((1, LANES), F32)],
        compiler_params=_cparams(("arbitrary", "arbitrary")),
        name="past_cache",
    )(k_past.reshape(b, p, WIDTH).astype(F32), v_past.reshape(b, p, WIDTH).astype(F32), lf, tri, place_k, ones_k)


def _attn_kernel(q_ref, k_ref, vt_ref, o_ref, m_ref, alpha_ref, s_ref, p_ref, acc_ref, *, tq, tk, past):
    qi = pl.program_id(2)
    q_first = past + qi * tq
    n_full = q_first // tk

    m_ref[...] = jnp.full_like(m_ref, -jnp.inf)
    alpha_ref[...] = jnp.ones_like(alpha_ref)
    p_ref[...] = jnp.zeros_like(p_ref)
    acc_ref[...] = jnp.zeros_like(acc_ref)

    def scores(j, slot):
        k_start = pl.multiple_of(j * tk, tk)
        for hb in range(2):
            s_ref[slot, hb] = lax.dot_general(k_ref[0, hb, pl.ds(k_start, tk), :], q_ref[0, hb],
                                              (((1,), (1,)), ((), ())), preferred_element_type=F32)

    def add_values(j, slot):
        k_start = pl.multiple_of(jnp.maximum(j, 0) * tk, tk)
        for hb in range(2):
            acc_ref[hb] = (alpha_ref[slot, hb] * acc_ref[hb]
                           + _dot(vt_ref[0, hb, :, pl.ds(k_start, tk)], p_ref[slot, hb]))

    def softmax(j, slot, masked):
        for hb in range(2):
            s = s_ref[slot, hb]
            if masked:
                k_pos = j * tk + lax.broadcasted_iota(jnp.int32, (tk, tq), 0)
                q_pos = q_first + lax.broadcasted_iota(jnp.int32, (tk, tq), 1)
                s = jnp.where(k_pos <= q_pos, s, -jnp.inf)
            m_prev = m_ref[hb]
            m_new = jnp.maximum(m_prev, jnp.max(s, axis=0, keepdims=True))
            alpha_ref[slot, hb] = jnp.exp2(m_prev - m_new)
            p_ref[slot, hb] = jnp.exp2(s - m_new).astype(BF16)
            m_ref[hb] = m_new

    def step(j, slot, last):
        add_values(j - 1, 1 - slot)
        softmax(j, slot, masked=last)
        if not last:
            scores(j + 1, 1 - slot)

    def block_pair(jj, carry):
        step(2 * jj, 0, False)
        step(2 * jj + 1, 1, False)
        return carry

    scores(0, 0)
    lax.fori_loop(0, n_full // 2, block_pair, 0)

    @pl.when(n_full % 2 == 1)
    def _():
        step(n_full - 1, 0, False)
        step(n_full, 1, True)
        add_values(n_full, 1)

    @pl.when(n_full % 2 == 0)
    def _():
        step(n_full, 0, True)
        add_values(n_full, 0)

    heads = [acc_ref[hb, 0:HEAD_DIM, :] / acc_ref[hb, HEAD_DIM:HEAD_DIM + 1, :] for hb in range(2)]
    o_ref[0] = jnp.concatenate(heads, axis=0).T.astype(o_ref.dtype)


def _fox_attention(q_aug, k_aug, v_t, past, tq, tk):
    b, _, t_q, _ = q_aug.shape
    t_k = k_aug.shape[2]
    assert t_q % tq == 0 and t_k % tk == 0 and tk % tq == 0 and past % tq == 0
    return pl.pallas_call(
        functools.partial(_attn_kernel, tq=tq, tk=tk, past=past),
        out_shape=jax.ShapeDtypeStruct((b, t_q, WIDTH), BF16),
        grid=(b, N_PAIRS, t_q // tq),
        in_specs=[pl.BlockSpec((1, 2, tq, LANES), lambda i, p, qi: (i, p, qi, 0)),
                  pl.BlockSpec((1, 2, t_k, LANES), lambda i, p, qi: (i, p, 0, 0)),
                  pl.BlockSpec((1, 2, V_ROWS, t_k), lambda i, p, qi: (i, p, 0, 0))],
        out_specs=pl.BlockSpec((1, tq, LANES), lambda i, p, qi: (i, qi, p)),
        scratch_shapes=[pltpu.VMEM((2, 1, tq), F32), pltpu.VMEM((2, 2, 1, tq), F32),
                        pltpu.VMEM((2, 2, tk, tq), F32), pltpu.VMEM((2, 2, tk, tq), BF16),
                        pltpu.VMEM((2, V_ROWS, tq), F32)],
        compiler_params=_cparams(("arbitrary", "arbitrary", "arbitrary")),
        name="fox_attention",
    )(q_aug, k_aug, v_t)


def _out_kernel(x_ref, ys_ref, g_ref, bonus_ref, yf_ref, mod_ref, gng_ref, gnb_ref, bd_ref, wo_ref, n2_ref,
                wg_ref, wu_ref, wd_ref, o_ref):
    bd = bd_ref[...]
    gt1 = mod_ref[0, 2:3, :]
    sh2 = mod_ref[0, 3:4, :]
    sc2 = mod_ref[0, 4:5, :]
    gt2 = mod_ref[0, 5:6, :]
    inv_hd = 1.0 / HEAD_DIM
    ys = ys_ref[0]
    d = ys - _split2_dot(ys, bd) * inv_hd
    var = _split2_dot(d * d, bd) * inv_hd
    yn = d * lax.rsqrt(var + GN_EPS) * gng_ref[...] + gnb_ref[...]
    y_rwkv = ((yn + bonus_ref[0]) * g_ref[0]).astype(BF16)
    mix = _dot(y_rwkv, wo_ref[0:WIDTH, :]) + _dot(yf_ref[0], wo_ref[WIDTH:2 * WIDTH, :])
    x1 = x_ref[0] + gt1 * mix
    h2 = x1 * lax.rsqrt(jnp.mean(x1 * x1, axis=-1, keepdims=True) + NORM_EPS) * n2_ref[...]
    hb = (h2 * (1.0 + sc2) + sh2).astype(BF16)
    act = (_silu(_dot(hb, wg_ref[...])) * _dot(hb, wu_ref[...])).astype(BF16)
    o_ref[0] = x1 + gt2 * _dot(act, wd_ref[...])


def _outproj_ffn(x, ys, g, bonus, yf, modp, wts, tm):
    b, t, _ = x.shape
    tile = lambda w: pl.BlockSpec((1, tm, w), lambda i, j: (i, j, 0))
    consts = [wts[n] for n in ("gn_g", "gn_b", "bd", "w_out", "norm2_g", "w_ffn_gate", "w_ffn_up", "w_ffn_down")]
    return pl.pallas_call(
        _out_kernel,
        out_shape=jax.ShapeDtypeStruct((b, t, D_MODEL), F32),
        grid=(b, t // tm),
        in_specs=([tile(D_MODEL), tile(WIDTH), tile(WIDTH), tile(WIDTH), tile(WIDTH),
                   pl.BlockSpec((1, 6, D_MODEL), lambda i, j: (i, 0, 0))]
                  + [_const_spec(c.shape) for c in consts]),
        out_specs=tile(D_MODEL),
        compiler_params=_cparams(("arbitrary", "arbitrary")),
        name="outproj_ffn",
    )(x, ys, g, bonus, yf, modp, *consts)


def _prep_weights(norm1_g, w_in, shift_mu, w0, w_decay_up, a0, w_aaa_up, w_gate_up, k_k, k_a, r_k, gn_g, gn_b,
                  fox_q_g, fox_k_g, fox_f_b, w_out, norm2_g, w_ffn_gate, w_ffn_up, w_ffn_down):
    row = lambda a: a.reshape(1, -1).astype(F32)
    fox0 = RWKV_COLS
    zeros_lora = jnp.zeros((DECAY_LORA, WIDTH), BF16)
    head = jnp.arange(WIDTH) // HEAD_DIM
    return {
        "norm1_g": row(norm1_g),
        "w_r": w_in[:, :RWKV_COLS].astype(BF16),
        "w_f": w_in[:, fox0:fox0 + 3 * WIDTH].astype(BF16),
        "w_fl": jnp.pad(w_in[:, fox0 + 3 * WIDTH:], ((0, 0), (0, LANES - N_HEADS))).astype(BF16),
        "shift_mu": row(shift_mu), "w0": row(w0), "a0": row(a0),
        "w_du": jnp.concatenate([w_decay_up.astype(BF16), zeros_lora], axis=0),
        "w_au": jnp.concatenate([zeros_lora, w_aaa_up.astype(BF16)], axis=0),
        "w_gu": w_gate_up.astype(BF16),
        "k_k": row(k_k), "k_a": row(k_a), "r_k": row(r_k),
        "fox_q_g": row(jnp.tile(fox_q_g, N_HEADS)), "fox_k_g": row(jnp.tile(fox_k_g, N_HEADS)),
        "fox_f_b": jnp.pad(row(fox_f_b), ((0, 0), (0, LANES - N_HEADS))),
        "bd": (head[:256, None] == head[None, :256]).astype(BF16),
        "gn_g": row(gn_g), "gn_b": row(gn_b),
        "w_out": w_out.astype(BF16), "norm2_g": row(norm2_g),
        "w_ffn_gate": w_ffn_gate.astype(BF16), "w_ffn_up": w_ffn_up.astype(BF16),
        "w_ffn_down": w_ffn_down.astype(BF16),
    }


def _to_pairs(state):
    b = state.shape[0]
    s = state.reshape(b, N_PAIRS, 2, HEAD_DIM, HEAD_DIM)
    return jnp.transpose(s, (0, 1, 4, 2, 3)).reshape(b, N_PAIRS, HEAD_DIM, LANES)


def _from_pairs(state):
    b = state.shape[0]
    s = state.reshape(b, N_PAIRS, HEAD_DIM, 2, HEAD_DIM)
    return jnp.transpose(s, (0, 1, 3, 4, 2)).reshape(b, N_HEADS, HEAD_DIM, HEAD_DIM)


def _round_up(n, m):
    return -(-n // m) * m


def _layer(x, modp, shift_prev, s_prev, k_past, v_past, logf_past, wts, tm, tq):
    b, t, _ = x.shape
    past = k_past.shape[1]
    if past:
        k_aug_past, v_t_past, lc_in = _past_cache(k_past, v_past, logf_past)
    else:
        lc_in = jnp.zeros((b, 1, LANES), F32)
    (r, w, k2, v, kk, bvec, g, bonus, q_aug, k_aug, v_t, kf, vf, lf_pad, new_shift) = _inproj(
        x, modp, shift_prev, lc_in, wts, tm)

    ys, s_new = _rwkv_scan(r, w, k2, v, kk, bvec, _to_pairs(s_prev.astype(F32)))

    logf = lf_pad[:, :, :N_HEADS]
    t_q = _round_up(t, LANES)
    q_aug = jnp.pad(q_aug, ((0, 0), (0, 0), (0, t_q - t), (0, 0)))
    tq = min(tq, t_q)
    if past:
        tk_all = _round_up(past + t_q, LANES)
        tail = tk_all - past - t
        k_aug = jnp.concatenate([k_aug_past, k_aug, jnp.zeros((b, N_HEADS, tail, LANES), BF16)], axis=2)
        v_t = jnp.concatenate([v_t_past, v_t, jnp.zeros((b, N_HEADS, V_ROWS, tail), BF16)], axis=3)
        tk = tk_all if t_q == tq else tq
    else:
        tk = max(tq, ATTN_KEY_BLOCK)
    yf = _fox_attention(q_aug, k_aug, v_t, past, tq, tk)[:, :t]

    y = _outproj_ffn(x, ys, g, bonus, yf, modp, wts, tm)
    return (y, _from_pairs(s_new), new_shift, kf.reshape(b, t, N_HEADS, HEAD_DIM),
            vf.reshape(b, t, N_HEADS, HEAD_DIM), logf)


def kernel(x_prompt, x_sample, cache_fox_k, cache_fox_v, cache_fox_logf, state_rwkv, state_rwkv_shift, c_prompt, c_sample, norm1_g, w_ada, b_ada, w_in, shift_mu, w0, w_decay_up, a0, w_aaa_up, w_gate_up, k_k, k_a, r_k, gn_g, gn_b, fox_q_g, fox_k_g, fox_f_b, w_out, norm2_g, w_ffn_gate, w_ffn_up, w_ffn_down):
    depth = w_in.shape[0]
    bp, bs = x_prompt.shape[0], x_sample.shape[0]
    zero_shift = jnp.zeros((bp, 1, RWKV_COLS), F32)
    zero_state = jnp.zeros((bp, N_HEADS, HEAD_DIM, HEAD_DIM), F32)
    zero_kv = jnp.zeros((bp, 0, N_HEADS, HEAD_DIM), F32)
    zero_logf = jnp.zeros((bp, 0, N_HEADS), F32)
    hp, hs = x_prompt, x_sample
    outs_p, outs_s = [], []
    for l in range(depth):
        wts = _prep_weights(norm1_g[l], w_in[l], shift_mu[l], w0[l], w_decay_up[l], a0[l], w_aaa_up[l],
                            w_gate_up[l], k_k[l], k_a[l], r_k[l], gn_g[l], gn_b[l], fox_q_g[l], fox_k_g[l],
                            fox_f_b[l], w_out[l], norm2_g[l], w_ffn_gate[l], w_ffn_up[l], w_ffn_down[l])
        rows = _round_up(bp + bs, 8)
        c_all = jnp.pad(jnp.concatenate([c_prompt, c_sample], axis=0), ((0, rows - bp - bs), (0, 0)))
        mod = _adaln(c_all, w_ada[l].astype(BF16), b_ada[l].reshape(1, -1)).reshape(rows, 6, D_MODEL)
        res_p = _layer(hp, mod[:bp], zero_shift, zero_state, zero_kv, zero_kv, zero_logf, wts, tm=256, tq=512)
        res_s = _layer(hs, mod[bp:bp + bs], state_rwkv_shift[l], state_rwkv[l], cache_fox_k[l], cache_fox_v[l],
                       cache_fox_logf[l], wts, tm=64, tq=512)
        hp, hs = res_p[0], res_s[0]
        outs_p.append(res_p[1:])
        outs_s.append(res_s[1:])
    stack = lambda outs, i: jnp.stack([o[i] for o in outs])
    return (hp, hs,
            stack(outs_p, 0), stack(outs_p, 1), stack(outs_p, 2), stack(outs_p, 3), stack(outs_p, 4),
            stack(outs_s, 0), stack(outs_s, 1), stack(outs_s, 2), stack(outs_s, 3), stack(outs_s, 4))
```

```python
import functools

import jax
import jax.numpy as jnp
from jax import lax
from jax.experimental import pallas as pl
from jax.experimental.pallas import tpu as pltpu

F32 = jnp.float32
BF16 = jnp.bfloat16

D_MODEL = 1024
HEAD_DIM = 64
N_HEADS = 8
WIDTH = N_HEADS * HEAD_DIM
N_PAIRS = N_HEADS // 2
DECAY_LORA = 64
AAA_LORA = 64
GATE_LORA = 128
RWKV_COLS = 3 * WIDTH + DECAY_LORA + AAA_LORA + GATE_LORA
LORA_OFF = 3 * WIDTH
NORM_EPS = 1e-6
GN_EPS = 64e-5
LANES = 128
SUBLANES = 8
SCAN_CHUNK = 512
ATTN_KEY_BLOCK = 512
PROJ_ROWS = 256
OUT_ROWS = 512
ATTN_QUERY_BLOCK = 512
V_ROWS = 80
LOG2E = 1.4426950408889634
VMEM_LIMIT = 56 * 1024 * 1024


def _cparams(sem):
    return pltpu.CompilerParams(dimension_semantics=sem, vmem_limit_bytes=VMEM_LIMIT)


def _const_spec(shape):
    nd = len(shape)
    return pl.BlockSpec(shape, lambda *_: (0,) * nd, pipeline_mode=pl.Buffered(1))


def _softplus(x):
    return jnp.maximum(x, 0.0) + jnp.log1p(jnp.exp(-jnp.abs(x)))


def _silu(x):
    return x * jax.nn.sigmoid(x)


def _dot(a, b):
    return jnp.dot(a, b, preferred_element_type=F32)


def _head_sums(x, m):
    hi = x.astype(BF16)
    lo = (x - hi.astype(F32)).astype(BF16)
    half = m.shape[0]
    parts = [_dot(hi[:, c:c + half], m) + _dot(lo[:, c:c + half], m) for c in range(0, x.shape[1], half)]
    return jnp.concatenate(parts, axis=-1)


def _split3_dot_rhs(m, x):
    hi = x.astype(BF16)
    r1 = x - hi.astype(F32)
    mid = r1.astype(BF16)
    lo = (r1 - mid.astype(F32)).astype(BF16)
    return _dot(m, hi) + _dot(m, mid) + _dot(m, lo)


def _mod_kernel(c_ref, w_ref, b_ref, o_ref):
    o_ref[...] = _dot(_silu(c_ref[...]).astype(BF16), w_ref[...]) + b_ref[...]


def _adaln(c_all, w_ada_bf, b_ada):
    rows = c_all.shape[0]
    n = w_ada_bf.shape[1]
    bn = D_MODEL
    return pl.pallas_call(
        _mod_kernel,
        out_shape=jax.ShapeDtypeStruct((rows, n), F32),
        grid=(n // bn,),
        in_specs=[pl.BlockSpec((rows, D_MODEL), lambda j: (0, 0)),
                  pl.BlockSpec((D_MODEL, bn), lambda j: (0, j)),
                  pl.BlockSpec((1, bn), lambda j: (0, j))],
        out_specs=pl.BlockSpec((rows, bn), lambda j: (0, j)),
        compiler_params=_cparams(("arbitrary",)),
        name="adaln_mod",
    )(c_all, w_ada_bf, b_ada)


def _inproj_kernel(x_ref, mod_ref, g1_ref, wr_ref, wf_ref, wfl_ref, sprev_ref, lcin_ref, mu_ref, w0_ref, a0_ref,
                   wdu_ref, wau_ref, wgu_ref, kk_ref, ka_ref, rk_ref, qg_ref, kg_ref, fb_ref, bd_ref,
                   tri_ref, pq_ref, pk_ref, oneq_ref, onek_ref,
                   r_o, w_o, k_o, v_o, kk_o, b_o, g_o, bonus_o, qa_o, ka_o, vt_o, kf_o, vf_o, lf_o, sh_o,
                   carry_ref, lc_ref, *, tm):
    t = pl.program_id(1)
    bd = bd_ref[...]
    x = x_ref[0]
    sh1 = mod_ref[0, 0:1, :]
    sc1 = mod_ref[0, 1:2, :]
    h = x * lax.rsqrt(jnp.mean(x * x, axis=-1, keepdims=True) + NORM_EPS) * g1_ref[...]
    hb = (h * (1.0 + sc1) + sh1).astype(BF16)
    pr = _dot(hb, wr_ref[...])
    pf = _dot(hb, wf_ref[...])
    pfl = _dot(hb, wfl_ref[...])

    @pl.when(t == 0)
    def _():
        carry_ref[...] = sprev_ref[0]

    prev = pltpu.roll(pr, 1, axis=0)
    first_row = lax.broadcasted_iota(jnp.int32, pr.shape, 0) == 0
    prev = jnp.where(first_row, carry_ref[...], prev)
    last = pr[tm - 1:tm, :]
    carry_ref[...] = last
    sh_o[0] = last
    z = pr + (prev - pr) * mu_ref[...]

    r = z[:, 0:WIDTH]
    k = z[:, WIDTH:2 * WIDTH]
    v = z[:, 2 * WIDTH:3 * WIDTH]
    dwa = z[:, LORA_OFF:LORA_OFF + LANES]
    dg = z[:, LORA_OFF + LANES:RWKV_COLS]
    w_log = -_softplus(-(w0_ref[...] + _dot(jnp.tanh(dwa).astype(BF16), wdu_ref[...]))) - 0.5
    decay = jnp.exp(-jnp.exp(w_log))
    a = jax.nn.sigmoid(a0_ref[...] + _dot(dwa.astype(BF16), wau_ref[...]))
    g = _dot(jax.nn.sigmoid(dg).astype(BF16), wgu_ref[...])
    kk = k * kk_ref[...]
    kk = kk / jnp.maximum(jnp.sqrt(_head_sums(kk * kk, bd)), 1e-12)
    k2 = k * (1.0 + (a - 1.0) * ka_ref[...])
    bonus = _head_sums(r * k2 * rk_ref[...], bd) * v

    r_o[0] = r
    w_o[0] = decay
    k_o[0] = k2
    v_o[0] = v
    kk_o[0] = kk
    b_o[0] = kk * a
    g_o[0] = g
    bonus_o[0] = bonus

    q = pf[:, 0:WIDTH]
    kf = pf[:, WIDTH:2 * WIDTH]
    vf = pf[:, 2 * WIDTH:3 * WIDTH]
    inv_hd = 1.0 / HEAD_DIM
    qn = q * lax.rsqrt(_head_sums(q * q, bd) * inv_hd + NORM_EPS) * qg_ref[...]
    kn = kf * lax.rsqrt(_head_sums(kf * kf, bd) * inv_hd + NORM_EPS) * kg_ref[...]
    kf_o[0] = kn
    vf_o[0] = vf
    logf = -_softplus(-(pfl + fb_ref[...]))
    lf_o[0] = logf

    @pl.when(t == 0)
    def _():
        lc_ref[...] = lcin_ref[0]

    lc = _running_log_forget(logf, tri_ref, lc_ref)
    q_bf = (qn * (HEAD_DIM ** -0.5 * LOG2E)).astype(BF16)
    _attention_operands(q_bf, None, lc, pq_ref, oneq_ref, qa_o, None, tm)
    _attention_operands(kn.astype(BF16), vf, lc, pk_ref, onek_ref, ka_o, vt_o, tm)


def _placement(sign_first):
    rows = jnp.arange(WIDTH + 3 * LANES)[:, None]
    cols = jnp.arange(N_HEADS * LANES)[None, :]
    c_head, c_lane = cols // LANES, cols % LANES
    feat = (rows < WIDTH) & (rows // HEAD_DIM == c_head) & (rows % HEAD_DIM == c_lane)
    piece, p_head = (rows - WIDTH) // LANES, (rows - WIDTH) % LANES
    lc_lane0 = HEAD_DIM if sign_first else HEAD_DIM + 3
    bias = (rows >= WIDTH) & (p_head == c_head) & (c_lane == lc_lane0 + piece)
    place = feat.astype(F32) + bias.astype(F32) * (1.0 if sign_first else -1.0)
    one_lane0 = HEAD_DIM + 3 if sign_first else HEAD_DIM
    ones = ((c_lane >= one_lane0) & (c_lane < one_lane0 + 3)).astype(F32)
    return place.astype(BF16), ones


def _inproj(x, modp, shift_prev, lc_in, wts, tm):
    b, t, _ = x.shape
    nt = t // tm
    tok = lambda w, dt: jax.ShapeDtypeStruct((b, t, w), dt)
    tile = lambda w: pl.BlockSpec((1, tm, w), lambda i, j: (i, j, 0))
    consts = [wts[n] for n in ("norm1_g", "w_r", "w_f", "w_fl")]
    consts2 = [wts[n] for n in ("shift_mu", "w0", "a0", "w_du", "w_au", "w_gu", "k_k", "k_a", "r_k",
                                "fox_q_g", "fox_k_g", "fox_f_b", "bd")]
    tri = (jnp.arange(tm)[:, None] >= jnp.arange(tm)[None, :]).astype(BF16)
    place_q, ones_q = _placement(True)
    place_k, ones_k = _placement(False)
    consts3 = [tri, place_q, place_k, ones_q, ones_k]
    per_stream = lambda w: pl.BlockSpec((1, 1, w), lambda i, j: (i, 0, 0))
    in_specs = ([tile(D_MODEL), pl.BlockSpec((1, 6, D_MODEL), lambda i, j: (i, 0, 0))]
                + [_const_spec(c.shape) for c in consts]
                + [per_stream(RWKV_COLS), per_stream(LANES)]
                + [_const_spec(c.shape) for c in consts2 + consts3])
    head_major = jax.ShapeDtypeStruct((b, N_HEADS, t, LANES), BF16)
    head_tile = pl.BlockSpec((1, N_HEADS, tm, LANES), lambda i, j: (i, 0, j, 0))
    out_shape = ([tok(WIDTH, F32)] * 8
                 + [head_major, head_major, jax.ShapeDtypeStruct((b, N_HEADS, V_ROWS, t), BF16)]
                 + [tok(WIDTH, F32)] * 2 + [tok(LANES, F32), jax.ShapeDtypeStruct((b, 1, RWKV_COLS), F32)])
    out_specs = ([tile(WIDTH)] * 8
                 + [head_tile, head_tile, pl.BlockSpec((1, N_HEADS, V_ROWS, tm), lambda i, j: (i, 0, 0, j))]
                 + [tile(WIDTH)] * 2 + [tile(LANES), per_stream(RWKV_COLS)])
    return pl.pallas_call(
        functools.partial(_inproj_kernel, tm=tm),
        out_shape=out_shape,
        grid=(b, nt),
        in_specs=in_specs,
        out_specs=out_specs,
        scratch_shapes=[pltpu.VMEM((1, RWKV_COLS), F32), pltpu.VMEM((1, LANES), F32)],
        compiler_params=_cparams(("arbitrary", "arbitrary")),
        name="inproj",
    )(x, modp, *consts, shift_prev, lc_in, *consts2, *consts3)


def _exact_pieces(x):
    hi = x.astype(BF16).astype(F32)
    r1 = x - hi
    mid = r1.astype(BF16).astype(F32)
    return hi, mid, r1 - mid


def _scan_kernel(r_ref, w_ref, k_ref, v_ref, kk_ref, b_ref, s0_ref, sel_ref, e_ref, fold_ref, y_ref, s_ref,
                 lhs_ref, tile_ref, rt_ref, gam_ref, *, n_steps):
    t = pl.program_id(1)

    @pl.when(t == 0)
    def _():
        s_ref[...] = s0_ref[...]

    sel = sel_ref[...]
    e_all = e_ref[...]
    row_id = lax.broadcasted_iota(jnp.int32, (SUBLANES, LANES), 0)
    lo8 = lax.broadcasted_iota(jnp.int32, (SUBLANES, LANES), 1) < HEAD_DIM
    n_vec = 5
    vec_cols = 6 * SUBLANES
    q_pad = jnp.zeros((2 * LANES - n_vec * vec_cols, LANES), F32)
    col_vec = lax.broadcasted_iota(jnp.int32, (1, 2 * LANES), 1) // vec_cols

    def transposed_halves(vectors):
        blocks = []
        for x8 in vectors:
            for h in range(2):
                blocks.extend(_exact_pieces(jnp.where(lo8 if h == 0 else jnp.logical_not(lo8), x8, 0.0)))
        q = jnp.concatenate(blocks + [q_pad], axis=0).astype(BF16)
        return lax.dot_general(sel, q, (((1,), (1,)), ((), ())), preferred_element_type=F32).astype(BF16)

    def shift_rows(x, sh, fill):
        return jnp.where(row_id >= sh, pltpu.roll(x, sh, axis=0), fill)

    n_groups = n_steps // SUBLANES
    last_lanes = (SUBLANES - 1) * LANES

    tile_rows = N_PAIRS * 3 * HEAD_DIM

    def transpose_group(group, slot):
        base = pl.multiple_of(jnp.minimum(group, n_groups - 1) * SUBLANES, SUBLANES)
        for p in range(N_PAIRS):
            rows8 = lambda ref: ref[0, pl.ds(base, SUBLANES), pl.ds(p * LANES, LANES)]
            g_inc = rows8(w_ref)
            for sh in (1, 2, 4):
                g_inc = g_inc * shift_rows(g_inc, sh, 1.0)
            g_exc = shift_rows(g_inc, 1, 1.0)
            inv = 1.0 / g_inc
            scaled = (rows8(kk_ref) * g_exc, rows8(b_ref) * inv, rows8(k_ref) * inv, rows8(r_ref) * g_inc, g_inc)
            lhs_all = transposed_halves(scaled)
            zero = jnp.zeros((), BF16)
            for vec in range(n_vec):
                if vec < 3:
                    row0 = (p * 3 + vec) * HEAD_DIM
                else:
                    row0 = tile_rows + ((vec - 3) * N_PAIRS + p) * HEAD_DIM
                lhs_ref[slot, row0:row0 + HEAD_DIM, :] = jnp.where(col_vec == vec, lhs_all, zero)

    pair_rows = N_PAIRS * HEAD_DIM

    def expand_group(slot_in, slot_out):
        tile_ref[slot_out] = _dot(lhs_ref[slot_in, 0:tile_rows, :], e_all)
        rt_ref[slot_out] = _dot(lhs_ref[slot_in, tile_rows:tile_rows + pair_rows, :], fold_ref[...])
        gam_ref[slot_out] = _dot(lhs_ref[slot_in, tile_rows + pair_rows:tile_rows + 2 * pair_rows, :],
                                 e_all[:, last_lanes:last_lanes + LANES])

    lane_half = lax.broadcasted_iota(jnp.int32, (HEAD_DIM, LANES), 1) // HEAD_DIM

    def run_tokens(group, slot, states):
        base = pl.multiple_of(group * SUBLANES, SUBLANES)
        states = list(states)
        v8 = [v_ref[0, pl.ds(base, SUBLANES), pl.ds(p * LANES, LANES)] for p in range(N_PAIRS)]
        ys = [[] for _ in range(N_PAIRS)]
        for u in range(SUBLANES):
            for p in range(N_PAIRS):
                def col(vec):
                    row0 = (p * 3 + vec) * HEAD_DIM
                    return tile_ref[slot, row0:row0 + HEAD_DIM, u * LANES:(u + 1) * LANES]
                s = states[p]
                sa = jnp.sum(s * col(0), axis=0, keepdims=True)
                s = s + col(2) * v8[p][u:u + 1, :] - col(1) * sa
                r_col = jnp.take_along_axis(rt_ref[slot, p * HEAD_DIM:(p + 1) * HEAD_DIM, :],
                                            lane_half * SUBLANES + u, axis=1)
                ys[p].append(jnp.sum(s * r_col, axis=0, keepdims=True))
                states[p] = s
        for p in range(N_PAIRS):
            states[p] = states[p] * gam_ref[slot, p * HEAD_DIM:(p + 1) * HEAD_DIM, :]
            y_ref[0, pl.ds(base, SUBLANES), pl.ds(p * LANES, LANES)] = jnp.concatenate(ys[p], axis=0)
        return tuple(states)

    def body(jj, states):
        g0 = 2 * jj
        states = run_tokens(g0, 0, states)
        expand_group(1, 1)
        transpose_group(g0 + 2, 0)
        states = run_tokens(g0 + 1, 1, states)
        expand_group(0, 0)
        transpose_group(g0 + 3, 1)
        return states

    transpose_group(0, 0)
    expand_group(0, 0)
    transpose_group(1, 1)
    init = tuple(s_ref[0, p] for p in range(N_PAIRS))
    final = lax.fori_loop(0, n_groups // 2, body, init)
    for p in range(N_PAIRS):
        s_ref[0, p] = final[p]


def _rwkv_scan(r, w, k, v, kk, bvec, s0_pairs):
    b, t, _ = r.shape
    chunk = min(SCAN_CHUNK, t)
    assert t % chunk == 0 and chunk % SUBLANES == 0
    nt = t // chunk
    lane = jnp.arange(LANES)
    sel = (lane[None, :] % HEAD_DIM == jnp.arange(HEAD_DIM)[:, None]).astype(BF16)
    q_row = jnp.arange(2 * LANES)
    q_half = (q_row % (6 * SUBLANES)) // (3 * SUBLANES)
    q_tok, q_used = q_row % SUBLANES, q_row < 5 * 6 * SUBLANES
    col = jnp.arange(SUBLANES * LANES)
    e_all = (q_used[:, None] & (q_tok[:, None] == col[None, :] // LANES)
             & (q_half[:, None] == (col[None, :] % LANES) // HEAD_DIM)).astype(BF16)
    fold = (q_used[:, None] & (lane[None, :] == q_half[:, None] * SUBLANES + q_tok[:, None])).astype(BF16)
    tile = pl.BlockSpec((1, chunk, WIDTH), lambda i, j: (i, j, 0))
    st = pl.BlockSpec((1, N_PAIRS, HEAD_DIM, LANES), lambda i, j: (i, 0, 0, 0))
    return pl.pallas_call(
        functools.partial(_scan_kernel, n_steps=chunk),
        out_shape=[jax.ShapeDtypeStruct((b, t, WIDTH), F32),
                   jax.ShapeDtypeStruct((b, N_PAIRS, HEAD_DIM, LANES), F32)],
        grid=(b, nt),
        in_specs=[tile] * 6 + [st, _const_spec(sel.shape), _const_spec(e_all.shape), _const_spec(fold.shape)],
        out_specs=[tile, st],
        scratch_shapes=[pltpu.VMEM((2, N_PAIRS * 5 * HEAD_DIM, 2 * LANES), BF16),
                        pltpu.VMEM((2, N_PAIRS * 3 * HEAD_DIM, SUBLANES * LANES), F32),
                        pltpu.VMEM((2, N_PAIRS * HEAD_DIM, LANES), F32),
                        pltpu.VMEM((2, N_PAIRS * HEAD_DIM, LANES), F32)],
        compiler_params=_cparams(("arbitrary", "arbitrary")),
        name="rwkv_scan",
    )(r, w, k, v, kk, bvec, s0_pairs, sel, e_all, fold)


def _attention_operands(feats_bf, v_f32, lc, place_ref, ones_ref, aug_o, vt_o, tm):
    lc = lc * LOG2E
    lc_hi = lc.astype(BF16)
    lc_r1 = lc - lc_hi.astype(F32)
    lc_mid = lc_r1.astype(BF16)
    lc_lo = (lc_r1 - lc_mid.astype(F32)).astype(BF16)
    aug = _dot(jnp.concatenate([feats_bf, lc_hi, lc_mid, lc_lo], axis=-1), place_ref[...]) + ones_ref[...]
    for hd in range(N_HEADS):
        aug_o[0, hd] = aug[:, hd * LANES:(hd + 1) * LANES].astype(BF16)
    if v_f32 is None:
        return
    if tm < LANES:
        v_f32 = jnp.concatenate([v_f32, jnp.zeros((LANES - tm, WIDTH), F32)], axis=0)
    v_t = v_f32.T[:, 0:tm].astype(BF16)
    extra_rows = lax.broadcasted_iota(jnp.int32, (V_ROWS - HEAD_DIM, tm), 0) == 0
    ones_then_zeros = jnp.where(extra_rows, 1.0, 0.0).astype(BF16)
    for hd in range(N_HEADS):
        vt_o[0, hd, 0:HEAD_DIM, :] = v_t[hd * HEAD_DIM:(hd + 1) * HEAD_DIM, :]
        vt_o[0, hd, HEAD_DIM:V_ROWS, :] = ones_then_zeros


def _running_log_forget(logf, tri_ref, lc_ref):
    head_lane = lax.broadcasted_iota(jnp.int32, (1, LANES), 1) < N_HEADS
    lc = _split3_dot_rhs(tri_ref[...], jnp.where(head_lane, logf, 0.0)) + lc_ref[...]
    lc_ref[...] = lc[lc.shape[0] - 1:, :]
    return lc


def _past_kernel(k_ref, v_ref, lf_ref, tri_ref, pk_ref, onek_ref, ka_o, vt_o, lc_o, lc_ref, *, tm):
    @pl.when(pl.program_id(1) == 0)
    def _():
        lc_ref[...] = jnp.zeros_like(lc_ref)

    lc = _running_log_forget(lf_ref[0], tri_ref, lc_ref)
    lc_o[0] = lc_ref[...]
    _attention_operands(k_ref[0].astype(BF16), v_ref[0], lc, pk_ref, onek_ref, ka_o, vt_o, tm)


def _past_cache(k_past, v_past, logf_past):
    b, p = k_past.shape[:2]
    tm = 512
    assert p % tm == 0
    tri = (jnp.arange(tm)[:, None] >= jnp.arange(tm)[None, :]).astype(BF16)
    place_k, ones_k = _placement(False)
    lf = jnp.pad(logf_past.astype(F32), ((0, 0), (0, 0), (0, LANES - N_HEADS)))
    rows = lambda w: pl.BlockSpec((1, tm, w), lambda i, j: (i, j, 0))
    return pl.pallas_call(
        functools.partial(_past_kernel, tm=tm),
        out_shape=[jax.ShapeDtypeStruct((b, N_HEADS, p, LANES), BF16),
                   jax.ShapeDtypeStruct((b, N_HEADS, V_ROWS, p), BF16),
                   jax.ShapeDtypeStruct((b, 1, LANES), F32)],
        grid=(b, p // tm),
        in_specs=[rows(WIDTH), rows(WIDTH), rows(LANES),
                  _const_spec(tri.shape), _const_spec(place_k.shape), _const_spec(ones_k.shape)],
        out_specs=[pl.BlockSpec((1, N_HEADS, tm, LANES), lambda i, j: (i, 0, j, 0)),
                   pl.BlockSpec((1, N_HEADS, V_ROWS, tm), lambda i, j: (i, 0, 0, j)),
                   pl.BlockSpec((1, 1, LANES), lambda i, j: (i, 0, 0))],
        scratch_shapes=[pltpu.VMEM((1, LANES), F32)],
        compiler_params=_cparams(("arbitrary", "arbitrary")),
        name="past_cache",
    )(k_past.reshape(b, p, WIDTH).astype(F32), v_past.reshape(b, p, WIDTH).astype(F32), lf, tri, place_k, ones_k)


def _attn_kernel(q_ref, k_ref, vt_ref, o_ref, m_ref, alpha_ref, s_ref, p_ref, acc_ref, *, tq, tk, past):
    qi = pl.program_id(2)
    q_first = past + qi * tq
    n_full = q_first // tk

    m_ref[...] = jnp.full_like(m_ref, -jnp.inf)
    alpha_ref[...] = jnp.ones_like(alpha_ref)
    p_ref[...] = jnp.zeros_like(p_ref)
    acc_ref[...] = jnp.zeros_like(acc_ref)

    def scores(j, slot):
        k_start = pl.multiple_of(j * tk, tk)
        for hb in range(2):
            s_ref[slot, hb] = lax.dot_general(k_ref[0, hb, pl.ds(k_start, tk), :], q_ref[0, hb],
                                              (((1,), (1,)), ((), ())), preferred_element_type=F32)

    def add_values(j, slot):
        k_start = pl.multiple_of(jnp.maximum(j, 0) * tk, tk)
        for hb in range(2):
            acc_ref[hb] = (alpha_ref[slot, hb] * acc_ref[hb]
                           + _dot(vt_ref[0, hb, :, pl.ds(k_start, tk)], p_ref[slot, hb]))

    def softmax(j, slot, masked):
        for hb in range(2):
            s = s_ref[slot, hb]
            if masked:
                k_pos = j * tk + lax.broadcasted_iota(jnp.int32, (tk, tq), 0)
                q_pos = q_first + lax.broadcasted_iota(jnp.int32, (tk, tq), 1)
                s = jnp.where(k_pos <= q_pos, s, -jnp.inf)
            m_prev = m_ref[hb]
            m_new = jnp.maximum(m_prev, jnp.max(s, axis=0, keepdims=True))
            alpha_ref[slot, hb] = jnp.exp2(m_prev - m_new)
            p_ref[slot, hb] = jnp.exp2(s - m_new).astype(BF16)
            m_ref[hb] = m_new

    def step(j, slot, last):
        add_values(j - 1, 1 - slot)
        softmax(j, slot, masked=last)
        if not last:
            scores(j + 1, 1 - slot)

    def block_pair(jj, carry):
        step(2 * jj, 0, False)
        step(2 * jj + 1, 1, False)
        return carry

    scores(0, 0)
    lax.fori_loop(0, n_full // 2, block_pair, 0)

    @pl.when(n_full % 2 == 1)
    def _():
        step(n_full - 1, 0, False)
        step(n_full, 1, True)
        add_values(n_full, 1)

    @pl.when(n_full % 2 == 0)
    def _():
        step(n_full, 0, True)
        add_values(n_full, 0)

    heads = [acc_ref[hb, 0:HEAD_DIM, :] / acc_ref[hb, HEAD_DIM:HEAD_DIM + 1, :] for hb in range(2)]
    o_ref[0] = jnp.concatenate(heads, axis=0).T.astype(o_ref.dtype)


def _fox_attention(q_aug, k_aug, v_t, past, tq, tk):
    b, _, t_q, _ = q_aug.shape
    t_k = k_aug.shape[2]
    assert t_q % tq == 0 and t_k % tk == 0 and tk % tq == 0 and past % tq == 0
    return pl.pallas_call(
        functools.partial(_attn_kernel, tq=tq, tk=tk, past=past),
        out_shape=jax.ShapeDtypeStruct((b, t_q, WIDTH), BF16),
        grid=(b, N_PAIRS, t_q // tq),
        in_specs=[pl.BlockSpec((1, 2, tq, LANES), lambda i, p, qi: (i, p, qi, 0)),
                  pl.BlockSpec((1, 2, t_k, LANES), lambda i, p, qi: (i, p, 0, 0)),
                  pl.BlockSpec((1, 2, V_ROWS, t_k), lambda i, p, qi: (i, p, 0, 0))],
        out_specs=pl.BlockSpec((1, tq, LANES), lambda i, p, qi: (i, qi, p)),
        scratch_shapes=[pltpu.VMEM((2, 1, tq), F32), pltpu.VMEM((2, 2, 1, tq), F32),
                        pltpu.VMEM((2, 2, tk, tq), F32), pltpu.VMEM((2, 2, tk, tq), BF16),
                        pltpu.VMEM((2, V_ROWS, tq), F32)],
        compiler_params=_cparams(("arbitrary", "arbitrary", "arbitrary")),
        name="fox_attention",
    )(q_aug, k_aug, v_t)


def _out_kernel(x_ref, ys_ref, g_ref, bonus_ref, yf_ref, mod_ref, gng_ref, gnb_ref, bd_ref, wo_ref, n2_ref,
                wg_ref, wu_ref, wd_ref, o_ref):
    bd = bd_ref[...]
    gt1 = mod_ref[0, 2:3, :]
    sh2 = mod_ref[0, 3:4, :]
    sc2 = mod_ref[0, 4:5, :]
    gt2 = mod_ref[0, 5:6, :]
    inv_hd = 1.0 / HEAD_DIM
    ys = ys_ref[0]
    d = ys - _head_sums(ys, bd) * inv_hd
    var = _head_sums(d * d, bd) * inv_hd
    yn = d * lax.rsqrt(var + GN_EPS) * gng_ref[...] + gnb_ref[...]
    y_rwkv = ((yn + bonus_ref[0]) * g_ref[0]).astype(BF16)
    mix = _dot(y_rwkv, wo_ref[0:WIDTH, :]) + _dot(yf_ref[0], wo_ref[WIDTH:2 * WIDTH, :])
    x1 = x_ref[0] + gt1 * mix
    h2 = x1 * lax.rsqrt(jnp.mean(x1 * x1, axis=-1, keepdims=True) + NORM_EPS) * n2_ref[...]
    hb = (h2 * (1.0 + sc2) + sh2).astype(BF16)
    act = (_silu(_dot(hb, wg_ref[...])) * _dot(hb, wu_ref[...])).astype(BF16)
    o_ref[0] = x1 + gt2 * _dot(act, wd_ref[...])


def _outproj_ffn(x, ys, g, bonus, yf, modp, wts, tm):
    b, t, _ = x.shape
    tile = lambda w: pl.BlockSpec((1, tm, w), lambda i, j: (i, j, 0))
    consts = [wts[n] for n in ("gn_g", "gn_b", "bd", "w_out", "norm2_g", "w_ffn_gate", "w_ffn_up", "w_ffn_down")]
    return pl.pallas_call(
        _out_kernel,
        out_shape=jax.ShapeDtypeStruct((b, t, D_MODEL), F32),
        grid=(b, t // tm),
        in_specs=([tile(D_MODEL), tile(WIDTH), tile(WIDTH), tile(WIDTH), tile(WIDTH),
                   pl.BlockSpec((1, 6, D_MODEL), lambda i, j: (i, 0, 0))]
                  + [_const_spec(c.shape) for c in consts]),
        out_specs=tile(D_MODEL),
        compiler_params=_cparams(("arbitrary", "arbitrary")),
        name="outproj_ffn",
    )(x, ys, g, bonus, yf, modp, *consts)


def _prep_weights(norm1_g, w_in, shift_mu, w0, w_decay_up, a0, w_aaa_up, w_gate_up, k_k, k_a, r_k, gn_g, gn_b,
                  fox_q_g, fox_k_g, fox_f_b, w_out, norm2_g, w_ffn_gate, w_ffn_up, w_ffn_down):
    row = lambda a: a.reshape(1, -1).astype(F32)
    fox0 = RWKV_COLS
    zeros_lora = jnp.zeros((DECAY_LORA, WIDTH), BF16)
    head = jnp.arange(WIDTH) // HEAD_DIM
    return {
        "norm1_g": row(norm1_g),
        "w_r": w_in[:, :RWKV_COLS].astype(BF16),
        "w_f": w_in[:, fox0:fox0 + 3 * WIDTH].astype(BF16),
        "w_fl": jnp.pad(w_in[:, fox0 + 3 * WIDTH:], ((0, 0), (0, LANES - N_HEADS))).astype(BF16),
        "shift_mu": row(shift_mu), "w0": row(w0), "a0": row(a0),
        "w_du": jnp.concatenate([w_decay_up.astype(BF16), zeros_lora], axis=0),
        "w_au": jnp.concatenate([zeros_lora, w_aaa_up.astype(BF16)], axis=0),
        "w_gu": w_gate_up.astype(BF16),
        "k_k": row(k_k), "k_a": row(k_a), "r_k": row(r_k),
        "fox_q_g": row(jnp.tile(fox_q_g, N_HEADS)), "fox_k_g": row(jnp.tile(fox_k_g, N_HEADS)),
        "fox_f_b": jnp.pad(row(fox_f_b), ((0, 0), (0, LANES - N_HEADS))),
        "bd": (head[:256, None] == head[None, :256]).astype(BF16),
        "gn_g": row(gn_g), "gn_b": row(gn_b),
        "w_out": w_out.astype(BF16), "norm2_g": row(norm2_g),
        "w_ffn_gate": w_ffn_gate.astype(BF16), "w_ffn_up": w_ffn_up.astype(BF16),
        "w_ffn_down": w_ffn_down.astype(BF16),
    }


def _to_pairs(state):
    b = state.shape[0]
    s = state.reshape(b, N_PAIRS, 2, HEAD_DIM, HEAD_DIM)
    return jnp.transpose(s, (0, 1, 4, 2, 3)).reshape(b, N_PAIRS, HEAD_DIM, LANES)


def _from_pairs(state):
    b = state.shape[0]
    s = state.reshape(b, N_PAIRS, HEAD_DIM, 2, HEAD_DIM)
    return jnp.transpose(s, (0, 1, 3, 4, 2)).reshape(b, N_HEADS, HEAD_DIM, HEAD_DIM)


def _round_up(n, m):
    return -(-n // m) * m


def _layer(x, modp, shift_prev, s_prev, k_past, v_past, logf_past, wts):
    b, t, _ = x.shape
    past = k_past.shape[1]
    tm, tm_out = min(PROJ_ROWS, t), min(OUT_ROWS, t)
    if past:
        k_aug_past, v_t_past, lc_in = _past_cache(k_past, v_past, logf_past)
    else:
        lc_in = jnp.zeros((b, 1, LANES), F32)
    (r, w, k2, v, kk, bvec, g, bonus, q_aug, k_aug, v_t, kf, vf, lf_pad, new_shift) = _inproj(
        x, modp, shift_prev, lc_in, wts, tm)

    ys, s_new = _rwkv_scan(r, w, k2, v, kk, bvec, _to_pairs(s_prev.astype(F32)))

    logf = lf_pad[:, :, :N_HEADS]
    t_q = _round_up(t, LANES)
    q_aug = jnp.pad(q_aug, ((0, 0), (0, 0), (0, t_q - t), (0, 0)))
    tq = min(ATTN_QUERY_BLOCK, t_q)
    if past:
        tk_all = _round_up(past + t_q, LANES)
        tail = tk_all - past - t
        k_aug = jnp.concatenate([k_aug_past, k_aug, jnp.zeros((b, N_HEADS, tail, LANES), BF16)], axis=2)
        v_t = jnp.concatenate([v_t_past, v_t, jnp.zeros((b, N_HEADS, V_ROWS, tail), BF16)], axis=3)
        tk = tk_all if t_q == tq else tq
    else:
        tk = max(tq, ATTN_KEY_BLOCK)
    yf = _fox_attention(q_aug, k_aug, v_t, past, tq, tk)[:, :t]

    y = _outproj_ffn(x, ys, g, bonus, yf, modp, wts, tm_out)
    return (y, _from_pairs(s_new), new_shift, kf.reshape(b, t, N_HEADS, HEAD_DIM),
            vf.reshape(b, t, N_HEADS, HEAD_DIM), logf)


def kernel(x_prompt, x_sample, cache_fox_k, cache_fox_v, cache_fox_logf, state_rwkv, state_rwkv_shift, c_prompt, c_sample, norm1_g, w_ada, b_ada, w_in, shift_mu, w0, w_decay_up, a0, w_aaa_up, w_gate_up, k_k, k_a, r_k, gn_g, gn_b, fox_q_g, fox_k_g, fox_f_b, w_out, norm2_g, w_ffn_gate, w_ffn_up, w_ffn_down):
    depth = w_in.shape[0]
    bp, bs = x_prompt.shape[0], x_sample.shape[0]
    zero_shift = jnp.zeros((bp, 1, RWKV_COLS), F32)
    zero_state = jnp.zeros((bp, N_HEADS, HEAD_DIM, HEAD_DIM), F32)
    zero_kv = jnp.zeros((bp, 0, N_HEADS, HEAD_DIM), F32)
    zero_logf = jnp.zeros((bp, 0, N_HEADS), F32)
    hp, hs = x_prompt, x_sample
    outs_p, outs_s = [], []
    for l in range(depth):
        wts = _prep_weights(norm1_g[l], w_in[l], shift_mu[l], w0[l], w_decay_up[l], a0[l], w_aaa_up[l],
                            w_gate_up[l], k_k[l], k_a[l], r_k[l], gn_g[l], gn_b[l], fox_q_g[l], fox_k_g[l],
                            fox_f_b[l], w_out[l], norm2_g[l], w_ffn_gate[l], w_ffn_up[l], w_ffn_down[l])
        rows = _round_up(bp + bs, SUBLANES)
        c_all = jnp.pad(jnp.concatenate([c_prompt, c_sample], axis=0), ((0, rows - bp - bs), (0, 0)))
        mod = _adaln(c_all, w_ada[l].astype(BF16), b_ada[l].reshape(1, -1)).reshape(rows, 6, D_MODEL)
        res_p = _layer(hp, mod[:bp], zero_shift, zero_state, zero_kv, zero_kv, zero_logf, wts)
        res_s = _layer(hs, mod[bp:bp + bs], state_rwkv_shift[l], state_rwkv[l], cache_fox_k[l], cache_fox_v[l],
                       cache_fox_logf[l], wts)
        hp, hs = res_p[0], res_s[0]
        outs_p.append(res_p[1:])
        outs_s.append(res_s[1:])
    stack = lambda outs, i: jnp.stack([o[i] for o in outs])
    return (hp, hs,
            stack(outs_p, 0), stack(outs_p, 1), stack(outs_p, 2), stack(outs_p, 3), stack(outs_p, 4),
            stack(outs_s, 0), stack(outs_s, 1), stack(outs_s, 2), stack(outs_s, 3), stack(outs_s, 4))
```

```python
import functools

import jax
import jax.numpy as jnp
from jax import lax
from jax.experimental import pallas as pl
from jax.experimental.pallas import tpu as pltpu

F32 = jnp.float32
BF16 = jnp.bfloat16

D_MODEL = 1024
HEAD_DIM = 64
N_HEADS = 8
WIDTH = N_HEADS * HEAD_DIM
N_PAIRS = N_HEADS // 2
DECAY_LORA = 64
AAA_LORA = 64
GATE_LORA = 128
RWKV_COLS = 3 * WIDTH + DECAY_LORA + AAA_LORA + GATE_LORA
LORA_OFF = 3 * WIDTH
NORM_EPS = 1e-6
GN_EPS = 64e-5
LANES = 128
SUBLANES = 8
SCAN_CHUNK = 512
ATTN_KEY_BLOCK = 512
PROJ_ROWS = 256
OUT_ROWS = 512
ATTN_QUERY_BLOCK = 512
V_ROWS = 80
LOG2E = 1.4426950408889634
VMEM_LIMIT = 56 * 1024 * 1024


def _cparams(sem):
    return pltpu.CompilerParams(dimension_semantics=sem, vmem_limit_bytes=VMEM_LIMIT)


def _const_spec(shape):
    nd = len(shape)
    return pl.BlockSpec(shape, lambda *_: (0,) * nd, pipeline_mode=pl.Buffered(1))


def _softplus(x):
    return jnp.maximum(x, 0.0) + jnp.log1p(jnp.exp(-jnp.abs(x)))


def _silu(x):
    return x * jax.nn.sigmoid(x)


def _dot(a, b):
    return jnp.dot(a, b, preferred_element_type=F32)


def _head_sums(x, m):
    hi = x.astype(BF16)
    lo = (x - hi.astype(F32)).astype(BF16)
    half = m.shape[0]
    parts = [_dot(hi[:, c:c + half], m) + _dot(lo[:, c:c + half], m) for c in range(0, x.shape[1], half)]
    return jnp.concatenate(parts, axis=-1)


def _split3_dot_rhs(m, x):
    hi = x.astype(BF16)
    r1 = x - hi.astype(F32)
    mid = r1.astype(BF16)
    lo = (r1 - mid.astype(F32)).astype(BF16)
    return _dot(m, hi) + _dot(m, mid) + _dot(m, lo)


def _mod_kernel(c_ref, w_ref, b_ref, o_ref):
    o_ref[...] = _dot(_silu(c_ref[...]).astype(BF16), w_ref[...]) + b_ref[...]


def _adaln(c_all, w_ada_bf, b_ada):
    rows = c_all.shape[0]
    n = w_ada_bf.shape[1]
    bn = D_MODEL
    return pl.pallas_call(
        _mod_kernel,
        out_shape=jax.ShapeDtypeStruct((rows, n), F32),
        grid=(n // bn,),
        in_specs=[pl.BlockSpec((rows, D_MODEL), lambda j: (0, 0)),
                  pl.BlockSpec((D_MODEL, bn), lambda j: (0, j)),
                  pl.BlockSpec((1, bn), lambda j: (0, j))],
        out_specs=pl.BlockSpec((rows, bn), lambda j: (0, j)),
        compiler_params=_cparams(("arbitrary",)),
        name="adaln_mod",
    )(c_all, w_ada_bf, b_ada)


def _inproj_kernel(x_ref, mod_ref, g1_ref, wr_ref, wf_ref, wfl_ref, sprev_ref, lcin_ref, mu_ref, w0_ref, a0_ref,
                   wdu_ref, wau_ref, wgu_ref, kk_ref, ka_ref, rk_ref, qg_ref, kg_ref, fb_ref, bd_ref,
                   tri_ref, pq_ref, pk_ref, oneq_ref, onek_ref,
                   r_o, w_o, k_o, v_o, kk_o, b_o, g_o, bonus_o, qa_o, ka_o, vt_o, kf_o, vf_o, lf_o, sh_o,
                   carry_ref, lc_ref, *, tm):
    t = pl.program_id(1)
    bd = bd_ref[...]
    x = x_ref[0]
    sh1 = mod_ref[0, 0:1, :]
    sc1 = mod_ref[0, 1:2, :]
    h = x * lax.rsqrt(jnp.mean(x * x, axis=-1, keepdims=True) + NORM_EPS) * g1_ref[...]
    hb = (h * (1.0 + sc1) + sh1).astype(BF16)
    pr = _dot(hb, wr_ref[...])
    pf = _dot(hb, wf_ref[...])
    pfl = _dot(hb, wfl_ref[...])

    @pl.when(t == 0)
    def _():
        carry_ref[...] = sprev_ref[0]

    prev = pltpu.roll(pr, 1, axis=0)
    first_row = lax.broadcasted_iota(jnp.int32, pr.shape, 0) == 0
    prev = jnp.where(first_row, carry_ref[...], prev)
    last = pr[tm - 1:tm, :]
    carry_ref[...] = last
    sh_o[0] = last
    z = pr + (prev - pr) * mu_ref[...]

    r = z[:, 0:WIDTH]
    k = z[:, WIDTH:2 * WIDTH]
    v = z[:, 2 * WIDTH:3 * WIDTH]
    dwa = z[:, LORA_OFF:LORA_OFF + LANES]
    dg = z[:, LORA_OFF + LANES:RWKV_COLS]
    w_log = -_softplus(-(w0_ref[...] + _dot(jnp.tanh(dwa).astype(BF16), wdu_ref[...]))) - 0.5
    decay = jnp.exp(-jnp.exp(w_log))
    a = jax.nn.sigmoid(a0_ref[...] + _dot(dwa.astype(BF16), wau_ref[...]))
    g = _dot(jax.nn.sigmoid(dg).astype(BF16), wgu_ref[...])
    kk = k * kk_ref[...]
    kk = kk / jnp.maximum(jnp.sqrt(_head_sums(kk * kk, bd)), 1e-12)
    k2 = k * (1.0 + (a - 1.0) * ka_ref[...])
    bonus = _head_sums(r * k2 * rk_ref[...], bd) * v

    r_o[0] = r
    w_o[0] = decay
    k_o[0] = k2
    v_o[0] = v
    kk_o[0] = kk
    b_o[0] = kk * a
    g_o[0] = g
    bonus_o[0] = bonus

    q = pf[:, 0:WIDTH]
    kf = pf[:, WIDTH:2 * WIDTH]
    vf = pf[:, 2 * WIDTH:3 * WIDTH]
    inv_hd = 1.0 / HEAD_DIM
    qn = q * lax.rsqrt(_head_sums(q * q, bd) * inv_hd + NORM_EPS) * qg_ref[...]
    kn = kf * lax.rsqrt(_head_sums(kf * kf, bd) * inv_hd + NORM_EPS) * kg_ref[...]
    kf_o[0] = kn
    vf_o[0] = vf
    logf = -_softplus(-(pfl + fb_ref[...]))
    lf_o[0] = logf

    @pl.when(t == 0)
    def _():
        lc_ref[...] = lcin_ref[0]

    lc = _running_log_forget(logf, tri_ref, lc_ref)
    q_bf = (qn * (HEAD_DIM ** -0.5 * LOG2E)).astype(BF16)
    _attention_operands(q_bf, None, lc, pq_ref, oneq_ref, qa_o, None, tm)
    _attention_operands(kn.astype(BF16), vf, lc, pk_ref, onek_ref, ka_o, vt_o, tm)


def _placement(sign_first):
    rows = jnp.arange(WIDTH + 3 * LANES)[:, None]
    cols = jnp.arange(N_HEADS * LANES)[None, :]
    c_head, c_lane = cols // LANES, cols % LANES
    feat = (rows < WIDTH) & (rows // HEAD_DIM == c_head) & (rows % HEAD_DIM == c_lane)
    piece, p_head = (rows - WIDTH) // LANES, (rows - WIDTH) % LANES
    lc_lane0 = HEAD_DIM if sign_first else HEAD_DIM + 3
    bias = (rows >= WIDTH) & (p_head == c_head) & (c_lane == lc_lane0 + piece)
    place = feat.astype(F32) + bias.astype(F32) * (1.0 if sign_first else -1.0)
    one_lane0 = HEAD_DIM + 3 if sign_first else HEAD_DIM
    ones = ((c_lane >= one_lane0) & (c_lane < one_lane0 + 3)).astype(F32)
    return place.astype(BF16), ones


def _inproj(x, modp, shift_prev, lc_in, wts, tm):
    b, t, _ = x.shape
    nt = t // tm
    tok = lambda w, dt: jax.ShapeDtypeStruct((b, t, w), dt)
    tile = lambda w: pl.BlockSpec((1, tm, w), lambda i, j: (i, j, 0))
    consts = [wts[n] for n in ("norm1_g", "w_r", "w_f", "w_fl")]
    consts2 = [wts[n] for n in ("shift_mu", "w0", "a0", "w_du", "w_au", "w_gu", "k_k", "k_a", "r_k",
                                "fox_q_g", "fox_k_g", "fox_f_b", "bd")]
    tri = (jnp.arange(tm)[:, None] >= jnp.arange(tm)[None, :]).astype(BF16)
    place_q, ones_q = _placement(True)
    place_k, ones_k = _placement(False)
    consts3 = [tri, place_q, place_k, ones_q, ones_k]
    per_stream = lambda w: pl.BlockSpec((1, 1, w), lambda i, j: (i, 0, 0))
    in_specs = ([tile(D_MODEL), pl.BlockSpec((1, 6, D_MODEL), lambda i, j: (i, 0, 0))]
                + [_const_spec(c.shape) for c in consts]
                + [per_stream(RWKV_COLS), per_stream(LANES)]
                + [_const_spec(c.shape) for c in consts2 + consts3])
    head_major = jax.ShapeDtypeStruct((b, N_HEADS, t, LANES), BF16)
    head_tile = pl.BlockSpec((1, N_HEADS, tm, LANES), lambda i, j: (i, 0, j, 0))
    out_shape = ([tok(WIDTH, F32)] * 8
                 + [head_major, head_major, jax.ShapeDtypeStruct((b, N_HEADS, V_ROWS, t), BF16)]
                 + [tok(WIDTH, F32)] * 2 + [tok(LANES, F32), jax.ShapeDtypeStruct((b, 1, RWKV_COLS), F32)])
    out_specs = ([tile(WIDTH)] * 8
                 + [head_tile, head_tile, pl.BlockSpec((1, N_HEADS, V_ROWS, tm), lambda i, j: (i, 0, 0, j))]
                 + [tile(WIDTH)] * 2 + [tile(LANES), per_stream(RWKV_COLS)])
    return pl.pallas_call(
        functools.partial(_inproj_kernel, tm=tm),
        out_shape=out_shape,
        grid=(b, nt),
        in_specs=in_specs,
        out_specs=out_specs,
        scratch_shapes=[pltpu.VMEM((1, RWKV_COLS), F32), pltpu.VMEM((1, LANES), F32)],
        compiler_params=_cparams(("arbitrary", "arbitrary")),
        name="inproj",
    )(x, modp, *consts, shift_prev, lc_in, *consts2, *consts3)


def _exact_pieces(x):
    hi = x.astype(BF16).astype(F32)
    r1 = x - hi
    mid = r1.astype(BF16).astype(F32)
    return hi, mid, r1 - mid


def _scan_kernel(r_ref, w_ref, k_ref, v_ref, kk_ref, b_ref, s0_ref, sel_ref, e_ref, fold_ref, y_ref, s_ref,
                 lhs_ref, tile_ref, rt_ref, gam_ref, *, n_steps):
    t = pl.program_id(1)

    @pl.when(t == 0)
    def _():
        s_ref[...] = s0_ref[...]

    sel = sel_ref[...]
    e_all = e_ref[...]
    row_id = lax.broadcasted_iota(jnp.int32, (SUBLANES, LANES), 0)
    lo8 = lax.broadcasted_iota(jnp.int32, (SUBLANES, LANES), 1) < HEAD_DIM
    n_vec = 5
    vec_cols = 6 * SUBLANES
    q_pad = jnp.zeros((2 * LANES - n_vec * vec_cols, LANES), F32)
    col_vec = lax.broadcasted_iota(jnp.int32, (1, 2 * LANES), 1) // vec_cols

    def transposed_halves(vectors):
        blocks = []
        for x8 in vectors:
            for h in range(2):
                blocks.extend(_exact_pieces(jnp.where(lo8 if h == 0 else jnp.logical_not(lo8), x8, 0.0)))
        q = jnp.concatenate(blocks + [q_pad], axis=0).astype(BF16)
        return lax.dot_general(sel, q, (((1,), (1,)), ((), ())), preferred_element_type=F32).astype(BF16)

    def shift_rows(x, sh, fill):
        return jnp.where(row_id >= sh, pltpu.roll(x, sh, axis=0), fill)

    n_groups = n_steps // SUBLANES
    last_lanes = (SUBLANES - 1) * LANES

    tile_rows = N_PAIRS * 3 * HEAD_DIM

    def transpose_group(group, slot):
        base = pl.multiple_of(jnp.minimum(group, n_groups - 1) * SUBLANES, SUBLANES)
        for p in range(N_PAIRS):
            rows8 = lambda ref: ref[0, pl.ds(base, SUBLANES), pl.ds(p * LANES, LANES)]
            g_inc = rows8(w_ref)
            for sh in (1, 2, 4):
                g_inc = g_inc * shift_rows(g_inc, sh, 1.0)
            g_exc = shift_rows(g_inc, 1, 1.0)
            inv = 1.0 / g_inc
            scaled = (rows8(kk_ref) * g_exc, rows8(b_ref) * inv, rows8(k_ref) * inv, rows8(r_ref) * g_inc, g_inc)
            lhs_all = transposed_halves(scaled)
            zero = jnp.zeros((), BF16)
            for vec in range(n_vec):
                if vec < 3:
                    row0 = (p * 3 + vec) * HEAD_DIM
                else:
                    row0 = tile_rows + ((vec - 3) * N_PAIRS + p) * HEAD_DIM
                lhs_ref[slot, row0:row0 + HEAD_DIM, :] = jnp.where(col_vec == vec, lhs_all, zero)

    pair_rows = N_PAIRS * HEAD_DIM

    def expand_group(slot_in, slot_out):
        tile_ref[slot_out] = _dot(lhs_ref[slot_in, 0:tile_rows, :], e_all)
        rt_ref[slot_out] = _dot(lhs_ref[slot_in, tile_rows:tile_rows + pair_rows, :], fold_ref[...])
        gam_ref[slot_out] = _dot(lhs_ref[slot_in, tile_rows + pair_rows:tile_rows + 2 * pair_rows, :],
                                 e_all[:, last_lanes:last_lanes + LANES])

    lane_half = lax.broadcasted_iota(jnp.int32, (HEAD_DIM, LANES), 1) // HEAD_DIM

    def run_tokens(group, slot, states):
        base = pl.multiple_of(group * SUBLANES, SUBLANES)
        states = list(states)
        v8 = [v_ref[0, pl.ds(base, SUBLANES), pl.ds(p * LANES, LANES)] for p in range(N_PAIRS)]
        ys = [[] for _ in range(N_PAIRS)]
        for u in range(SUBLANES):
            for p in range(N_PAIRS):
                def col(vec):
                    row0 = (p * 3 + vec) * HEAD_DIM
                    return tile_ref[slot, row0:row0 + HEAD_DIM, u * LANES:(u + 1) * LANES]
                s = states[p]
                sa = jnp.sum(s * col(0), axis=0, keepdims=True)
                s = s + col(2) * v8[p][u:u + 1, :] - col(1) * sa
                r_col = jnp.take_along_axis(rt_ref[slot, p * HEAD_DIM:(p + 1) * HEAD_DIM, :],
                                            lane_half * SUBLANES + u, axis=1)
                ys[p].append(jnp.sum(s * r_col, axis=0, keepdims=True))
                states[p] = s
        for p in range(N_PAIRS):
            states[p] = states[p] * gam_ref[slot, p * HEAD_DIM:(p + 1) * HEAD_DIM, :]
            y_ref[0, pl.ds(base, SUBLANES), pl.ds(p * LANES, LANES)] = jnp.concatenate(ys[p], axis=0)
        return tuple(states)

    def body(jj, states):
        g0 = 2 * jj
        states = run_tokens(g0, 0, states)
        expand_group(1, 1)
        transpose_group(g0 + 2, 0)
        states = run_tokens(g0 + 1, 1, states)
        expand_group(0, 0)
        transpose_group(g0 + 3, 1)
        return states

    transpose_group(0, 0)
    expand_group(0, 0)
    transpose_group(1, 1)
    init = tuple(s_ref[0, p] for p in range(N_PAIRS))
    final = lax.fori_loop(0, n_groups // 2, body, init)
    for p in range(N_PAIRS):
        s_ref[0, p] = final[p]


def _rwkv_scan(r, w, k, v, kk, bvec, s0_pairs):
    b, t, _ = r.shape
    chunk = min(SCAN_CHUNK, t)
    assert t % chunk == 0 and chunk % SUBLANES == 0
    nt = t // chunk
    lane = jnp.arange(LANES)
    sel = (lane[None, :] % HEAD_DIM == jnp.arange(HEAD_DIM)[:, None]).astype(BF16)
    q_row = jnp.arange(2 * LANES)
    q_half = (q_row % (6 * SUBLANES)) // (3 * SUBLANES)
    q_tok, q_used = q_row % SUBLANES, q_row < 5 * 6 * SUBLANES
    col = jnp.arange(SUBLANES * LANES)
    e_all = (q_used[:, None] & (q_tok[:, None] == col[None, :] // LANES)
             & (q_half[:, None] == (col[None, :] % LANES) // HEAD_DIM)).astype(BF16)
    fold = (q_used[:, None] & (lane[None, :] == q_half[:, None] * SUBLANES + q_tok[:, None])).astype(BF16)
    tile = pl.BlockSpec((1, chunk, WIDTH), lambda i, j: (i, j, 0))
    st = pl.BlockSpec((1, N_PAIRS, HEAD_DIM, LANES), lambda i, j: (i, 0, 0, 0))
    return pl.pallas_call(
        functools.partial(_scan_kernel, n_steps=chunk),
        out_shape=[jax.ShapeDtypeStruct((b, t, WIDTH), F32),
                   jax.ShapeDtypeStruct((b, N_PAIRS, HEAD_DIM, LANES), F32)],
        grid=(b, nt),
        in_specs=[tile] * 6 + [st, _const_spec(sel.shape), _const_spec(e_all.shape), _const_spec(fold.shape)],
        out_specs=[tile, st],
        scratch_shapes=[pltpu.VMEM((2, N_PAIRS * 5 * HEAD_DIM, 2 * LANES), BF16),
                        pltpu.VMEM((2, N_PAIRS * 3 * HEAD_DIM, SUBLANES * LANES), F32),
                        pltpu.VMEM((2, N_PAIRS * HEAD_DIM, LANES), F32),
                        pltpu.VMEM((2, N_PAIRS * HEAD_DIM, LANES), F32)],
        compiler_params=_cparams(("arbitrary", "arbitrary")),
        name="rwkv_scan",
    )(r, w, k, v, kk, bvec, s0_pairs, sel, e_all, fold)


def _attention_operands(feats_bf, v_f32, lc, place_ref, ones_ref, aug_o, vt_o, tm):
    lc = lc * LOG2E
    lc_hi = lc.astype(BF16)
    lc_r1 = lc - lc_hi.astype(F32)
    lc_mid = lc_r1.astype(BF16)
    lc_lo = (lc_r1 - lc_mid.astype(F32)).astype(BF16)
    half = WIDTH // 2
    bias = _dot(jnp.concatenate([lc_hi, lc_mid, lc_lo], axis=-1), place_ref[WIDTH:, :]) + ones_ref[...]
    aug = jnp.concatenate(
        [_dot(feats_bf[:, c:c + half], place_ref[c:c + half, 2 * c:2 * c + 2 * half]) + bias[:, 2 * c:2 * c + 2 * half]
         for c in (0, half)], axis=-1)
    for hd in range(N_HEADS):
        aug_o[0, hd] = aug[:, hd * LANES:(hd + 1) * LANES].astype(BF16)
    if v_f32 is None:
        return
    if tm < LANES:
        v_f32 = jnp.concatenate([v_f32, jnp.zeros((LANES - tm, WIDTH), F32)], axis=0)
    v_t = v_f32.T[:, 0:tm].astype(BF16)
    extra_rows = lax.broadcasted_iota(jnp.int32, (V_ROWS - HEAD_DIM, tm), 0) == 0
    ones_then_zeros = jnp.where(extra_rows, 1.0, 0.0).astype(BF16)
    for hd in range(N_HEADS):
        vt_o[0, hd, 0:HEAD_DIM, :] = v_t[hd * HEAD_DIM:(hd + 1) * HEAD_DIM, :]
        vt_o[0, hd, HEAD_DIM:V_ROWS, :] = ones_then_zeros


def _running_log_forget(logf, tri_ref, lc_ref):
    head_lane = lax.broadcasted_iota(jnp.int32, (1, LANES), 1) < N_HEADS
    lc = _split3_dot_rhs(tri_ref[...], jnp.where(head_lane, logf, 0.0)) + lc_ref[...]
    lc_ref[...] = lc[lc.shape[0] - 1:, :]
    return lc


def _past_kernel(k_ref, v_ref, lf_ref, tri_ref, pk_ref, onek_ref, ka_o, vt_o, lc_o, lc_ref, *, tm):
    @pl.when(pl.program_id(1) == 0)
    def _():
        lc_ref[...] = jnp.zeros_like(lc_ref)

    lc = _running_log_forget(lf_ref[0], tri_ref, lc_ref)
    lc_o[0] = lc_ref[...]
    _attention_operands(k_ref[0].astype(BF16), v_ref[0], lc, pk_ref, onek_ref, ka_o, vt_o, tm)


def _past_cache(k_past, v_past, logf_past):
    b, p = k_past.shape[:2]
    tm = 512
    assert p % tm == 0
    tri = (jnp.arange(tm)[:, None] >= jnp.arange(tm)[None, :]).astype(BF16)
    place_k, ones_k = _placement(False)
    lf = jnp.pad(logf_past.astype(F32), ((0, 0), (0, 0), (0, LANES - N_HEADS)))
    rows = lambda w: pl.BlockSpec((1, tm, w), lambda i, j: (i, j, 0))
    return pl.pallas_call(
        functools.partial(_past_kernel, tm=tm),
        out_shape=[jax.ShapeDtypeStruct((b, N_HEADS, p, LANES), BF16),
                   jax.ShapeDtypeStruct((b, N_HEADS, V_ROWS, p), BF16),
                   jax.ShapeDtypeStruct((b, 1, LANES), F32)],
        grid=(b, p // tm),
        in_specs=[rows(WIDTH), rows(WIDTH), rows(LANES),
                  _const_spec(tri.shape), _const_spec(place_k.shape), _const_spec(ones_k.shape)],
        out_specs=[pl.BlockSpec((1, N_HEADS, tm, LANES), lambda i, j: (i, 0, j, 0)),
                   pl.BlockSpec((1, N_HEADS, V_ROWS, tm), lambda i, j: (i, 0, 0, j)),
                   pl.BlockSpec((1, 1, LANES), lambda i, j: (i, 0, 0))],
        scratch_shapes=[pltpu.VMEM((1, LANES), F32)],
        compiler_params=_cparams(("arbitrary", "arbitrary")),
        name="past_cache",
    )(k_past.reshape(b, p, WIDTH).astype(F32), v_past.reshape(b, p, WIDTH).astype(F32), lf, tri, place_k, ones_k)


def _attn_kernel(q_ref, k_ref, vt_ref, o_ref, m_ref, alpha_ref, s_ref, p_ref, acc_ref, *, tq, tk, past):
    qi = pl.program_id(2)
    q_first = past + qi * tq
    n_full = q_first // tk

    m_ref[...] = jnp.full_like(m_ref, -jnp.inf)
    alpha_ref[...] = jnp.ones_like(alpha_ref)
    p_ref[1] = jnp.zeros(p_ref.shape[1:], p_ref.dtype)
    acc_ref[...] = jnp.zeros_like(acc_ref)

    def scores(j, slot):
        k_start = pl.multiple_of(j * tk, tk)
        for hb in range(2):
            s_ref[slot, hb] = lax.dot_general(k_ref[0, hb, pl.ds(k_start, tk), :], q_ref[0, hb],
                                              (((1,), (1,)), ((), ())), preferred_element_type=F32)

    def add_values(j, slot):
        k_start = pl.multiple_of(jnp.maximum(j, 0) * tk, tk)
        for hb in range(2):
            acc_ref[hb] = (alpha_ref[slot, hb] * acc_ref[hb]
                           + _dot(vt_ref[0, hb, :, pl.ds(k_start, tk)], p_ref[slot, hb]))

    def softmax(j, slot, masked):
        for hb in range(2):
            s = s_ref[slot, hb]
            if masked:
                k_pos = j * tk + lax.broadcasted_iota(jnp.int32, (tk, tq), 0)
                q_pos = q_first + lax.broadcasted_iota(jnp.int32, (tk, tq), 1)
                s = jnp.where(k_pos <= q_pos, s, -jnp.inf)
            m_prev = m_ref[hb]
            m_new = jnp.maximum(m_prev, jnp.max(s, axis=0, keepdims=True))
            alpha_ref[slot, hb] = jnp.exp2(m_prev - m_new)
            p_ref[slot, hb] = jnp.exp2(s - m_new).astype(BF16)
            m_ref[hb] = m_new

    def step(j, slot, last):
        add_values(j - 1, 1 - slot)
        softmax(j, slot, masked=last)
        if not last:
            scores(j + 1, 1 - slot)

    def block_pair(jj, carry):
        step(2 * jj, 0, False)
        step(2 * jj + 1, 1, False)
        return carry

    scores(0, 0)
    lax.fori_loop(0, n_full // 2, block_pair, 0)

    @pl.when(n_full % 2 == 1)
    def _():
        step(n_full - 1, 0, False)
        step(n_full, 1, True)
        add_values(n_full, 1)

    @pl.when(n_full % 2 == 0)
    def _():
        step(n_full, 0, True)
        add_values(n_full, 0)

    heads = [acc_ref[hb, 0:HEAD_DIM, :] / acc_ref[hb, HEAD_DIM:HEAD_DIM + 1, :] for hb in range(2)]
    o_ref[0] = jnp.concatenate(heads, axis=0).T.astype(o_ref.dtype)


def _fox_attention(q_aug, k_aug, v_t, past, tq, tk):
    b, _, t_q, _ = q_aug.shape
    t_k = k_aug.shape[2]
    assert t_q % tq == 0 and t_k % tk == 0 and tk % tq == 0 and past % tq == 0
    return pl.pallas_call(
        functools.partial(_attn_kernel, tq=tq, tk=tk, past=past),
        out_shape=jax.ShapeDtypeStruct((b, t_q, WIDTH), BF16),
        grid=(b, N_PAIRS, t_q // tq),
        in_specs=[pl.BlockSpec((1, 2, tq, LANES), lambda i, p, qi: (i, p, qi, 0)),
                  pl.BlockSpec((1, 2, t_k, LANES), lambda i, p, qi: (i, p, 0, 0)),
                  pl.BlockSpec((1, 2, V_ROWS, t_k), lambda i, p, qi: (i, p, 0, 0))],
        out_specs=pl.BlockSpec((1, tq, LANES), lambda i, p, qi: (i, qi, p)),
        scratch_shapes=[pltpu.VMEM((2, 1, tq), F32), pltpu.VMEM((2, 2, 1, tq), F32),
                        pltpu.VMEM((2, 2, tk, tq), F32), pltpu.VMEM((2, 2, tk, tq), BF16),
                        pltpu.VMEM((2, V_ROWS, tq), F32)],
        compiler_params=_cparams(("arbitrary", "arbitrary", "arbitrary")),
        name="fox_attention",
    )(q_aug, k_aug, v_t)


def _out_kernel(x_ref, ys_ref, g_ref, bonus_ref, yf_ref, mod_ref, gng_ref, gnb_ref, bd_ref, wo_ref, n2_ref,
                wg_ref, wu_ref, wd_ref, o_ref):
    bd = bd_ref[...]
    gt1 = mod_ref[0, 2:3, :]
    sh2 = mod_ref[0, 3:4, :]
    sc2 = mod_ref[0, 4:5, :]
    gt2 = mod_ref[0, 5:6, :]
    inv_hd = 1.0 / HEAD_DIM
    ys = ys_ref[0]
    d = ys - _head_sums(ys, bd) * inv_hd
    var = _head_sums(d * d, bd) * inv_hd
    yn = d * lax.rsqrt(var + GN_EPS) * gng_ref[...] + gnb_ref[...]
    y_rwkv = ((yn + bonus_ref[0]) * g_ref[0]).astype(BF16)
    mix = _dot(y_rwkv, wo_ref[0:WIDTH, :]) + _dot(yf_ref[0], wo_ref[WIDTH:2 * WIDTH, :])
    x1 = x_ref[0] + gt1 * mix
    h2 = x1 * lax.rsqrt(jnp.mean(x1 * x1, axis=-1, keepdims=True) + NORM_EPS) * n2_ref[...]
    hb = (h2 * (1.0 + sc2) + sh2).astype(BF16)
    act = (_silu(_dot(hb, wg_ref[...])) * _dot(hb, wu_ref[...])).astype(BF16)
    o_ref[0] = x1 + gt2 * _dot(act, wd_ref[...])


def _outproj_ffn(x, ys, g, bonus, yf, modp, wts, tm):
    b, t, _ = x.shape
    tile = lambda w: pl.BlockSpec((1, tm, w), lambda i, j: (i, j, 0))
    consts = [wts[n] for n in ("gn_g", "gn_b", "bd", "w_out", "norm2_g", "w_ffn_gate", "w_ffn_up", "w_ffn_down")]
    return pl.pallas_call(
        _out_kernel,
        out_shape=jax.ShapeDtypeStruct((b, t, D_MODEL), F32),
        grid=(b, t // tm),
        in_specs=([tile(D_MODEL), tile(WIDTH), tile(WIDTH), tile(WIDTH), tile(WIDTH),
                   pl.BlockSpec((1, 6, D_MODEL), lambda i, j: (i, 0, 0))]
                  + [_const_spec(c.shape) for c in consts]),
        out_specs=tile(D_MODEL),
        compiler_params=_cparams(("arbitrary", "arbitrary")),
        name="outproj_ffn",
    )(x, ys, g, bonus, yf, modp, *consts)


def _prep_weights(norm1_g, w_in, shift_mu, w0, w_decay_up, a0, w_aaa_up, w_gate_up, k_k, k_a, r_k, gn_g, gn_b,
                  fox_q_g, fox_k_g, fox_f_b, w_out, norm2_g, w_ffn_gate, w_ffn_up, w_ffn_down):
    row = lambda a: a.reshape(1, -1).astype(F32)
    fox0 = RWKV_COLS
    zeros_lora = jnp.zeros((DECAY_LORA, WIDTH), BF16)
    head = jnp.arange(WIDTH) // HEAD_DIM
    return {
        "norm1_g": row(norm1_g),
        "w_r": w_in[:, :RWKV_COLS].astype(BF16),
        "w_f": w_in[:, fox0:fox0 + 3 * WIDTH].astype(BF16),
        "w_fl": jnp.pad(w_in[:, fox0 + 3 * WIDTH:], ((0, 0), (0, LANES - N_HEADS))).astype(BF16),
        "shift_mu": row(shift_mu), "w0": row(w0), "a0": row(a0),
        "w_du": jnp.concatenate([w_decay_up.astype(BF16), zeros_lora], axis=0),
        "w_au": jnp.concatenate([zeros_lora, w_aaa_up.astype(BF16)], axis=0),
        "w_gu": w_gate_up.astype(BF16),
        "k_k": row(k_k), "k_a": row(k_a), "r_k": row(r_k),
        "fox_q_g": row(jnp.tile(fox_q_g, N_HEADS)), "fox_k_g": row(jnp.tile(fox_k_g, N_HEADS)),
        "fox_f_b": jnp.pad(row(fox_f_b), ((0, 0), (0, LANES - N_HEADS))),
        "bd": (head[:256, None] == head[None, :256]).astype(BF16),
        "gn_g": row(gn_g), "gn_b": row(gn_b),
        "w_out": w_out.astype(BF16), "norm2_g": row(norm2_g),
        "w_ffn_gate": w_ffn_gate.astype(BF16), "w_ffn_up": w_ffn_up.astype(BF16),
        "w_ffn_down": w_ffn_down.astype(BF16),
    }


def _to_pairs(state):
    b = state.shape[0]
    s = state.reshape(b, N_PAIRS, 2, HEAD_DIM, HEAD_DIM)
    return jnp.transpose(s, (0, 1, 4, 2, 3)).reshape(b, N_PAIRS, HEAD_DIM, LANES)


def _from_pairs(state):
    b = state.shape[0]
    s = state.reshape(b, N_PAIRS, HEAD_DIM, 2, HEAD_DIM)
    return jnp.transpose(s, (0, 1, 3, 4, 2)).reshape(b, N_HEADS, HEAD_DIM, HEAD_DIM)


def _round_up(n, m):
    return -(-n // m) * m


def _layer(x, modp, shift_prev, s_prev, k_past, v_past, logf_past, wts):
    b, t, _ = x.shape
    past = k_past.shape[1]
    tm, tm_out = min(PROJ_ROWS, t), min(OUT_ROWS, t)
    if past:
        k_aug_past, v_t_past, lc_in = _past_cache(k_past, v_past, logf_past)
    else:
        lc_in = jnp.zeros((b, 1, LANES), F32)
    (r, w, k2, v, kk, bvec, g, bonus, q_aug, k_aug, v_t, kf, vf, lf_pad, new_shift) = _inproj(
        x, modp, shift_prev, lc_in, wts, tm)

    ys, s_new = _rwkv_scan(r, w, k2, v, kk, bvec, _to_pairs(s_prev.astype(F32)))

    logf = lf_pad[:, :, :N_HEADS]
    t_q = _round_up(t, LANES)
    q_aug = jnp.pad(q_aug, ((0, 0), (0, 0), (0, t_q - t), (0, 0)))
    tq = min(ATTN_QUERY_BLOCK, t_q)
    if past:
        tk_all = _round_up(past + t_q, LANES)
        tail = tk_all - past - t
        k_aug = jnp.concatenate([k_aug_past, k_aug, jnp.zeros((b, N_HEADS, tail, LANES), BF16)], axis=2)
        v_t = jnp.concatenate([v_t_past, v_t, jnp.zeros((b, N_HEADS, V_ROWS, tail), BF16)], axis=3)
        tk = tk_all if t_q == tq else tq
    else:
        tk = max(tq, ATTN_KEY_BLOCK)
    yf = _fox_attention(q_aug, k_aug, v_t, past, tq, tk)[:, :t]

    y = _outproj_ffn(x, ys, g, bonus, yf, modp, wts, tm_out)
    return (y, _from_pairs(s_new), new_shift, kf.reshape(b, t, N_HEADS, HEAD_DIM),
            vf.reshape(b, t, N_HEADS, HEAD_DIM), logf)


def kernel(x_prompt, x_sample, cache_fox_k, cache_fox_v, cache_fox_logf, state_rwkv, state_rwkv_shift, c_prompt, c_sample, norm1_g, w_ada, b_ada, w_in, shift_mu, w0, w_decay_up, a0, w_aaa_up, w_gate_up, k_k, k_a, r_k, gn_g, gn_b, fox_q_g, fox_k_g, fox_f_b, w_out, norm2_g, w_ffn_gate, w_ffn_up, w_ffn_down):
    depth = w_in.shape[0]
    bp, bs = x_prompt.shape[0], x_sample.shape[0]
    zero_shift = jnp.zeros((bp, 1, RWKV_COLS), F32)
    zero_state = jnp.zeros((bp, N_HEADS, HEAD_DIM, HEAD_DIM), F32)
    zero_kv = jnp.zeros((bp, 0, N_HEADS, HEAD_DIM), F32)
    zero_logf = jnp.zeros((bp, 0, N_HEADS), F32)
    hp, hs = x_prompt, x_sample
    outs_p, outs_s = [], []
    for l in range(depth):
        wts = _prep_weights(norm1_g[l], w_in[l], shift_mu[l], w0[l], w_decay_up[l], a0[l], w_aaa_up[l],
                            w_gate_up[l], k_k[l], k_a[l], r_k[l], gn_g[l], gn_b[l], fox_q_g[l], fox_k_g[l],
                            fox_f_b[l], w_out[l], norm2_g[l], w_ffn_gate[l], w_ffn_up[l], w_ffn_down[l])
        rows = _round_up(bp + bs, SUBLANES)
        c_all = jnp.pad(jnp.concatenate([c_prompt, c_sample], axis=0), ((0, rows - bp - bs), (0, 0)))
        mod = _adaln(c_all, w_ada[l].astype(BF16), b_ada[l].reshape(1, -1)).reshape(rows, 6, D_MODEL)
        res_p = _layer(hp, mod[:bp], zero_shift, zero_state, zero_kv, zero_kv, zero_logf, wts)
        res_s = _layer(hs, mod[bp:bp + bs], state_rwkv_shift[l], state_rwkv[l], cache_fox_k[l], cache_fox_v[l],
                       cache_fox_logf[l], wts)
        hp, hs = res_p[0], res_s[0]
        outs_p.append(res_p[1:])
        outs_s.append(res_s[1:])
    stack = lambda outs, i: jnp.stack([o[i] for o in outs])
    return (hp, hs,
            stack(outs_p, 0), stack(outs_p, 1), stack(outs_p, 2), stack(outs_p, 3), stack(outs_p, 4),
            stack(outs_s, 0), stack(outs_s, 1), stack(outs_s, 2), stack(outs_s, 3), stack(outs_s, 4))
```

```python
import functools

import jax
import jax.numpy as jnp
from jax import lax
from jax.experimental import pallas as pl
from jax.experimental.pallas import tpu as pltpu

F32 = jnp.float32
BF16 = jnp.bfloat16

D_MODEL = 1024
HEAD_DIM = 64
N_HEADS = 8
WIDTH = N_HEADS * HEAD_DIM
N_PAIRS = N_HEADS // 2
DECAY_LORA = 64
AAA_LORA = 64
GATE_LORA = 128
RWKV_COLS = 3 * WIDTH + DECAY_LORA + AAA_LORA + GATE_LORA
LORA_OFF = 3 * WIDTH
NORM_EPS = 1e-6
GN_EPS = 64e-5
LANES = 128
SUBLANES = 8
SCAN_CHUNK = 1024
ATTN_KEY_BLOCK = 512
PROJ_ROWS = 256
OUT_ROWS = 512
ATTN_QUERY_BLOCK = 512
V_ROWS = 80
LOG2E = 1.4426950408889634
VMEM_LIMIT = 56 * 1024 * 1024


def _cparams(sem):
    return pltpu.CompilerParams(dimension_semantics=sem, vmem_limit_bytes=VMEM_LIMIT)


def _const_spec(shape):
    nd = len(shape)
    return pl.BlockSpec(shape, lambda *_: (0,) * nd, pipeline_mode=pl.Buffered(1))


def _softplus(x):
    return jnp.maximum(x, 0.0) + jnp.log1p(jnp.exp(-jnp.abs(x)))


def _silu(x):
    return x * jax.nn.sigmoid(x)


def _dot(a, b):
    return jnp.dot(a, b, preferred_element_type=F32)


def _head_sums(x, m):
    hi = x.astype(BF16)
    lo = (x - hi.astype(F32)).astype(BF16)
    half = m.shape[0]
    parts = [_dot(hi[:, c:c + half], m) + _dot(lo[:, c:c + half], m) for c in range(0, x.shape[1], half)]
    return jnp.concatenate(parts, axis=-1)


def _split3_dot_rhs(m, x):
    hi = x.astype(BF16)
    r1 = x - hi.astype(F32)
    mid = r1.astype(BF16)
    lo = (r1 - mid.astype(F32)).astype(BF16)
    return _dot(m, hi) + _dot(m, mid) + _dot(m, lo)


def _mod_kernel(c_ref, w_ref, b_ref, o_ref):
    o_ref[...] = _dot(_silu(c_ref[...]).astype(BF16), w_ref[...]) + b_ref[...]


def _adaln(c_all, w_ada_bf, b_ada):
    rows = c_all.shape[0]
    n = w_ada_bf.shape[1]
    bn = D_MODEL
    return pl.pallas_call(
        _mod_kernel,
        out_shape=jax.ShapeDtypeStruct((rows, n), F32),
        grid=(n // bn,),
        in_specs=[pl.BlockSpec((rows, D_MODEL), lambda j: (0, 0)),
                  pl.BlockSpec((D_MODEL, bn), lambda j: (0, j)),
                  pl.BlockSpec((1, bn), lambda j: (0, j))],
        out_specs=pl.BlockSpec((rows, bn), lambda j: (0, j)),
        compiler_params=_cparams(("arbitrary",)),
        name="adaln_mod",
    )(c_all, w_ada_bf, b_ada)


def _inproj_kernel(x_ref, mod_ref, g1_ref, wr_ref, wf_ref, wfl_ref, sprev_ref, lcin_ref, mu_ref, w0_ref, a0_ref,
                   wdu_ref, wau_ref, wgu_ref, kk_ref, ka_ref, rk_ref, qg_ref, kg_ref, fb_ref, bd_ref,
                   tri_ref, pq_ref, pk_ref, oneq_ref, onek_ref,
                   r_o, w_o, k_o, v_o, kk_o, b_o, g_o, bonus_o, qa_o, ka_o, vt_o, kf_o, vf_o, lf_o, sh_o,
                   carry_ref, lc_ref, *, tm):
    t = pl.program_id(1)
    bd = bd_ref[...]
    x = x_ref[0]
    sh1 = mod_ref[0, 0:1, :]
    sc1 = mod_ref[0, 1:2, :]
    h = x * lax.rsqrt(jnp.mean(x * x, axis=-1, keepdims=True) + NORM_EPS) * g1_ref[...]
    hb = (h * (1.0 + sc1) + sh1).astype(BF16)
    pr = _dot(hb, wr_ref[...])
    pf = _dot(hb, wf_ref[...])
    pfl = _dot(hb, wfl_ref[...])

    @pl.when(t == 0)
    def _():
        carry_ref[...] = sprev_ref[0]

    prev = pltpu.roll(pr, 1, axis=0)
    first_row = lax.broadcasted_iota(jnp.int32, pr.shape, 0) == 0
    prev = jnp.where(first_row, carry_ref[...], prev)
    last = pr[tm - 1:tm, :]
    carry_ref[...] = last
    sh_o[0] = last
    z = pr + (prev - pr) * mu_ref[...]

    r = z[:, 0:WIDTH]
    k = z[:, WIDTH:2 * WIDTH]
    v = z[:, 2 * WIDTH:3 * WIDTH]
    dwa = z[:, LORA_OFF:LORA_OFF + LANES]
    dg = z[:, LORA_OFF + LANES:RWKV_COLS]
    w_log = -_softplus(-(w0_ref[...] + _dot(jnp.tanh(dwa).astype(BF16), wdu_ref[...]))) - 0.5
    decay = jnp.exp(-jnp.exp(w_log))
    a = jax.nn.sigmoid(a0_ref[...] + _dot(dwa.astype(BF16), wau_ref[...]))
    g = _dot(jax.nn.sigmoid(dg).astype(BF16), wgu_ref[...])
    kk = k * kk_ref[...]
    kk = kk / jnp.maximum(jnp.sqrt(_head_sums(kk * kk, bd)), 1e-12)
    k2 = k * (1.0 + (a - 1.0) * ka_ref[...])
    bonus = _head_sums(r * k2 * rk_ref[...], bd) * v

    r_o[0] = r
    w_o[0] = decay
    k_o[0] = k2
    v_o[0] = v
    kk_o[0] = kk
    b_o[0] = kk * a
    g_o[0] = g
    bonus_o[0] = bonus

    q = pf[:, 0:WIDTH]
    kf = pf[:, WIDTH:2 * WIDTH]
    vf = pf[:, 2 * WIDTH:3 * WIDTH]
    inv_hd = 1.0 / HEAD_DIM
    qn = q * lax.rsqrt(_head_sums(q * q, bd) * inv_hd + NORM_EPS) * qg_ref[...]
    kn = kf * lax.rsqrt(_head_sums(kf * kf, bd) * inv_hd + NORM_EPS) * kg_ref[...]
    kf_o[0] = kn
    vf_o[0] = vf
    logf = -_softplus(-(pfl + fb_ref[...]))
    lf_o[0] = logf

    @pl.when(t == 0)
    def _():
        lc_ref[...] = lcin_ref[0]

    lc = _running_log_forget(logf, tri_ref, lc_ref)
    q_bf = (qn * (HEAD_DIM ** -0.5 * LOG2E)).astype(BF16)
    _attention_operands(q_bf, None, lc, pq_ref, oneq_ref, qa_o, None, tm)
    _attention_operands(kn.astype(BF16), vf, lc, pk_ref, onek_ref, ka_o, vt_o, tm)


def _placement(sign_first):
    rows = jnp.arange(WIDTH + 3 * LANES)[:, None]
    cols = jnp.arange(N_HEADS * LANES)[None, :]
    c_head, c_lane = cols // LANES, cols % LANES
    feat = (rows < WIDTH) & (rows // HEAD_DIM == c_head) & (rows % HEAD_DIM == c_lane)
    piece, p_head = (rows - WIDTH) // LANES, (rows - WIDTH) % LANES
    lc_lane0 = HEAD_DIM if sign_first else HEAD_DIM + 3
    bias = (rows >= WIDTH) & (p_head == c_head) & (c_lane == lc_lane0 + piece)
    place = feat.astype(F32) + bias.astype(F32) * (1.0 if sign_first else -1.0)
    one_lane0 = HEAD_DIM + 3 if sign_first else HEAD_DIM
    ones = ((c_lane >= one_lane0) & (c_lane < one_lane0 + 3)).astype(F32)
    return place.astype(BF16), ones


def _inproj(x, modp, shift_prev, lc_in, wts, tm):
    b, t, _ = x.shape
    nt = t // tm
    tok = lambda w, dt: jax.ShapeDtypeStruct((b, t, w), dt)
    tile = lambda w: pl.BlockSpec((1, tm, w), lambda i, j: (i, j, 0))
    consts = [wts[n] for n in ("norm1_g", "w_r", "w_f", "w_fl")]
    consts2 = [wts[n] for n in ("shift_mu", "w0", "a0", "w_du", "w_au", "w_gu", "k_k", "k_a", "r_k",
                                "fox_q_g", "fox_k_g", "fox_f_b", "bd")]
    tri = (jnp.arange(tm)[:, None] >= jnp.arange(tm)[None, :]).astype(BF16)
    place_q, ones_q = _placement(True)
    place_k, ones_k = _placement(False)
    consts3 = [tri, place_q, place_k, ones_q, ones_k]
    per_stream = lambda w: pl.BlockSpec((1, 1, w), lambda i, j: (i, 0, 0))
    in_specs = ([tile(D_MODEL), pl.BlockSpec((1, 6, D_MODEL), lambda i, j: (i, 0, 0))]
                + [_const_spec(c.shape) for c in consts]
                + [per_stream(RWKV_COLS), per_stream(LANES)]
                + [_const_spec(c.shape) for c in consts2 + consts3])
    head_major = jax.ShapeDtypeStruct((b, N_HEADS, t, LANES), BF16)
    head_tile = pl.BlockSpec((1, N_HEADS, tm, LANES), lambda i, j: (i, 0, j, 0))
    out_shape = ([tok(WIDTH, F32)] * 8
                 + [head_major, head_major, jax.ShapeDtypeStruct((b, N_HEADS, V_ROWS, t), BF16)]
                 + [tok(WIDTH, F32)] * 2 + [tok(LANES, F32), jax.ShapeDtypeStruct((b, 1, RWKV_COLS), F32)])
    out_specs = ([tile(WIDTH)] * 8
                 + [head_tile, head_tile, pl.BlockSpec((1, N_HEADS, V_ROWS, tm), lambda i, j: (i, 0, 0, j))]
                 + [tile(WIDTH)] * 2 + [tile(LANES), per_stream(RWKV_COLS)])
    return pl.pallas_call(
        functools.partial(_inproj_kernel, tm=tm),
        out_shape=out_shape,
        grid=(b, nt),
        in_specs=in_specs,
        out_specs=out_specs,
        scratch_shapes=[pltpu.VMEM((1, RWKV_COLS), F32), pltpu.VMEM((1, LANES), F32)],
        compiler_params=_cparams(("arbitrary", "arbitrary")),
        name="inproj",
    )(x, modp, *consts, shift_prev, lc_in, *consts2, *consts3)


def _exact_pieces(x):
    hi = x.astype(BF16).astype(F32)
    r1 = x - hi
    mid = r1.astype(BF16).astype(F32)
    return hi, mid, r1 - mid


def _scan_kernel(r_ref, w_ref, k_ref, v_ref, kk_ref, b_ref, s0_ref, sel_ref, e_ref, fold_ref, y_ref, s_ref,
                 lhs_ref, tile_ref, rt_ref, gam_ref, *, n_steps):
    t = pl.program_id(1)

    @pl.when(t == 0)
    def _():
        s_ref[...] = s0_ref[...]

    sel = sel_ref[...]
    e_all = e_ref[...]
    row_id = lax.broadcasted_iota(jnp.int32, (SUBLANES, LANES), 0)
    lo8 = lax.broadcasted_iota(jnp.int32, (SUBLANES, LANES), 1) < HEAD_DIM
    n_vec = 5
    vec_cols = 6 * SUBLANES
    q_pad = jnp.zeros((2 * LANES - n_vec * vec_cols, LANES), F32)
    col_vec = lax.broadcasted_iota(jnp.int32, (1, 2 * LANES), 1) // vec_cols

    def transposed_halves(vectors):
        blocks = []
        for x8 in vectors:
            for h in range(2):
                blocks.extend(_exact_pieces(jnp.where(lo8 if h == 0 else jnp.logical_not(lo8), x8, 0.0)))
        q = jnp.concatenate(blocks + [q_pad], axis=0).astype(BF16)
        return lax.dot_general(sel, q, (((1,), (1,)), ((), ())), preferred_element_type=F32).astype(BF16)

    def shift_rows(x, sh, fill):
        return jnp.where(row_id >= sh, pltpu.roll(x, sh, axis=0), fill)

    n_groups = n_steps // SUBLANES
    last_lanes = (SUBLANES - 1) * LANES

    tile_rows = N_PAIRS * 3 * HEAD_DIM

    def transpose_group(group, slot):
        base = pl.multiple_of(jnp.minimum(group, n_groups - 1) * SUBLANES, SUBLANES)
        for p in range(N_PAIRS):
            rows8 = lambda ref: ref[0, pl.ds(base, SUBLANES), pl.ds(p * LANES, LANES)]
            g_inc = rows8(w_ref)
            for sh in (1, 2, 4):
                g_inc = g_inc * shift_rows(g_inc, sh, 1.0)
            g_exc = shift_rows(g_inc, 1, 1.0)
            inv = 1.0 / g_inc
            scaled = (rows8(kk_ref) * g_exc, rows8(b_ref) * inv, rows8(k_ref) * inv, rows8(r_ref) * g_inc, g_inc)
            lhs_all = transposed_halves(scaled)
            zero = jnp.zeros((), BF16)
            for vec in range(n_vec):
                if vec < 3:
                    row0 = (p * 3 + vec) * HEAD_DIM
                else:
                    row0 = tile_rows + ((vec - 3) * N_PAIRS + p) * HEAD_DIM
                lhs_ref[slot, row0:row0 + HEAD_DIM, :] = jnp.where(col_vec == vec, lhs_all, zero)

    pair_rows = N_PAIRS * HEAD_DIM

    def expand_group(slot_in, slot_out):
        tile_ref[slot_out] = _dot(lhs_ref[slot_in, 0:tile_rows, :], e_all)
        rt_ref[slot_out] = _dot(lhs_ref[slot_in, tile_rows:tile_rows + pair_rows, :], fold_ref[...])
        gam_ref[slot_out] = _dot(lhs_ref[slot_in, tile_rows + pair_rows:tile_rows + 2 * pair_rows, :],
                                 e_all[:, last_lanes:last_lanes + LANES])

    lane_half = lax.broadcasted_iota(jnp.int32, (HEAD_DIM, LANES), 1) // HEAD_DIM

    def run_tokens(group, slot, states):
        base = pl.multiple_of(group * SUBLANES, SUBLANES)
        states = list(states)
        v8 = [v_ref[0, pl.ds(base, SUBLANES), pl.ds(p * LANES, LANES)] for p in range(N_PAIRS)]
        ys = [[] for _ in range(N_PAIRS)]
        for u in range(SUBLANES):
            for p in range(N_PAIRS):
                def col(vec):
                    row0 = (p * 3 + vec) * HEAD_DIM
                    return tile_ref[slot, row0:row0 + HEAD_DIM, u * LANES:(u + 1) * LANES]
                s = states[p]
                sa = jnp.sum(s * col(0), axis=0, keepdims=True)
                s = s + col(2) * v8[p][u:u + 1, :] - col(1) * sa
                r_col = jnp.take_along_axis(rt_ref[slot, p * HEAD_DIM:(p + 1) * HEAD_DIM, :],
                                            lane_half * SUBLANES + u, axis=1)
                ys[p].append(jnp.sum(s * r_col, axis=0, keepdims=True))
                states[p] = s
        for p in range(N_PAIRS):
            states[p] = states[p] * gam_ref[slot, p * HEAD_DIM:(p + 1) * HEAD_DIM, :]
            y_ref[0, pl.ds(base, SUBLANES), pl.ds(p * LANES, LANES)] = jnp.concatenate(ys[p], axis=0)
        return tuple(states)

    def body(jj, states):
        g0 = 2 * jj
        states = run_tokens(g0, 0, states)
        expand_group(1, 1)
        transpose_group(g0 + 2, 0)
        states = run_tokens(g0 + 1, 1, states)
        expand_group(0, 0)
        transpose_group(g0 + 3, 1)
        return states

    transpose_group(0, 0)
    expand_group(0, 0)
    transpose_group(1, 1)
    init = tuple(s_ref[0, p] for p in range(N_PAIRS))
    final = lax.fori_loop(0, n_groups // 2, body, init)
    for p in range(N_PAIRS):
        s_ref[0, p] = final[p]


def _rwkv_scan(r, w, k, v, kk, bvec, s0_pairs):
    b, t, _ = r.shape
    chunk = min(SCAN_CHUNK, t)
    assert t % chunk == 0 and chunk % SUBLANES == 0
    nt = t // chunk
    lane = jnp.arange(LANES)
    sel = (lane[None, :] % HEAD_DIM == jnp.arange(HEAD_DIM)[:, None]).astype(BF16)
    q_row = jnp.arange(2 * LANES)
    q_half = (q_row % (6 * SUBLANES)) // (3 * SUBLANES)
    q_tok, q_used = q_row % SUBLANES, q_row < 5 * 6 * SUBLANES
    col = jnp.arange(SUBLANES * LANES)
    e_all = (q_used[:, None] & (q_tok[:, None] == col[None, :] // LANES)
             & (q_half[:, None] == (col[None, :] % LANES) // HEAD_DIM)).astype(BF16)
    fold = (q_used[:, None] & (lane[None, :] == q_half[:, None] * SUBLANES + q_tok[:, None])).astype(BF16)
    tile = pl.BlockSpec((1, chunk, WIDTH), lambda i, j: (i, j, 0))
    st = pl.BlockSpec((1, N_PAIRS, HEAD_DIM, LANES), lambda i, j: (i, 0, 0, 0))
    return pl.pallas_call(
        functools.partial(_scan_kernel, n_steps=chunk),
        out_shape=[jax.ShapeDtypeStruct((b, t, WIDTH), F32),
                   jax.ShapeDtypeStruct((b, N_PAIRS, HEAD_DIM, LANES), F32)],
        grid=(b, nt),
        in_specs=[tile] * 6 + [st, _const_spec(sel.shape), _const_spec(e_all.shape), _const_spec(fold.shape)],
        out_specs=[tile, st],
        scratch_shapes=[pltpu.VMEM((2, N_PAIRS * 5 * HEAD_DIM, 2 * LANES), BF16),
                        pltpu.VMEM((2, N_PAIRS * 3 * HEAD_DIM, SUBLANES * LANES), F32),
                        pltpu.VMEM((2, N_PAIRS * HEAD_DIM, LANES), F32),
                        pltpu.VMEM((2, N_PAIRS * HEAD_DIM, LANES), F32)],
        compiler_params=_cparams(("arbitrary", "arbitrary")),
        name="rwkv_scan",
    )(r, w, k, v, kk, bvec, s0_pairs, sel, e_all, fold)


def _attention_operands(feats_bf, v_f32, lc, place_ref, ones_ref, aug_o, vt_o, tm):
    lc = lc * LOG2E
    lc_hi = lc.astype(BF16)
    lc_r1 = lc - lc_hi.astype(F32)
    lc_mid = lc_r1.astype(BF16)
    lc_lo = (lc_r1 - lc_mid.astype(F32)).astype(BF16)
    half = WIDTH // 2
    bias = _dot(jnp.concatenate([lc_hi, lc_mid, lc_lo], axis=-1), place_ref[WIDTH:, :]) + ones_ref[...]
    aug = jnp.concatenate(
        [_dot(feats_bf[:, c:c + half], place_ref[c:c + half, 2 * c:2 * c + 2 * half]) + bias[:, 2 * c:2 * c + 2 * half]
         for c in (0, half)], axis=-1)
    for hd in range(N_HEADS):
        aug_o[0, hd] = aug[:, hd * LANES:(hd + 1) * LANES].astype(BF16)
    if v_f32 is None:
        return
    if tm < LANES:
        v_f32 = jnp.concatenate([v_f32, jnp.zeros((LANES - tm, WIDTH), F32)], axis=0)
    v_t = v_f32.T[:, 0:tm].astype(BF16)
    extra_rows = lax.broadcasted_iota(jnp.int32, (V_ROWS - HEAD_DIM, tm), 0) == 0
    ones_then_zeros = jnp.where(extra_rows, 1.0, 0.0).astype(BF16)
    for hd in range(N_HEADS):
        vt_o[0, hd, 0:HEAD_DIM, :] = v_t[hd * HEAD_DIM:(hd + 1) * HEAD_DIM, :]
        vt_o[0, hd, HEAD_DIM:V_ROWS, :] = ones_then_zeros


def _running_log_forget(logf, tri_ref, lc_ref):
    head_lane = lax.broadcasted_iota(jnp.int32, (1, LANES), 1) < N_HEADS
    lc = _split3_dot_rhs(tri_ref[...], jnp.where(head_lane, logf, 0.0)) + lc_ref[...]
    lc_ref[...] = lc[lc.shape[0] - 1:, :]
    return lc


def _past_kernel(k_ref, v_ref, lf_ref, tri_ref, pk_ref, onek_ref, ka_o, vt_o, lc_o, lc_ref, *, tm):
    @pl.when(pl.program_id(1) == 0)
    def _():
        lc_ref[...] = jnp.zeros_like(lc_ref)

    lc = _running_log_forget(lf_ref[0], tri_ref, lc_ref)
    lc_o[0] = lc_ref[...]
    _attention_operands(k_ref[0].astype(BF16), v_ref[0], lc, pk_ref, onek_ref, ka_o, vt_o, tm)


def _past_cache(k_past, v_past, logf_past):
    b, p = k_past.shape[:2]
    tm = 512
    assert p % tm == 0
    tri = (jnp.arange(tm)[:, None] >= jnp.arange(tm)[None, :]).astype(BF16)
    place_k, ones_k = _placement(False)
    lf = jnp.pad(logf_past.astype(F32), ((0, 0), (0, 0), (0, LANES - N_HEADS)))
    rows = lambda w: pl.BlockSpec((1, tm, w), lambda i, j: (i, j, 0))
    return pl.pallas_call(
        functools.partial(_past_kernel, tm=tm),
        out_shape=[jax.ShapeDtypeStruct((b, N_HEADS, p, LANES), BF16),
                   jax.ShapeDtypeStruct((b, N_HEADS, V_ROWS, p), BF16),
                   jax.ShapeDtypeStruct((b, 1, LANES), F32)],
        grid=(b, p // tm),
        in_specs=[rows(WIDTH), rows(WIDTH), rows(LANES),
                  _const_spec(tri.shape), _const_spec(place_k.shape), _const_spec(ones_k.shape)],
        out_specs=[pl.BlockSpec((1, N_HEADS, tm, LANES), lambda i, j: (i, 0, j, 0)),
                   pl.BlockSpec((1, N_HEADS, V_ROWS, tm), lambda i, j: (i, 0, 0, j)),
                   pl.BlockSpec((1, 1, LANES), lambda i, j: (i, 0, 0))],
        scratch_shapes=[pltpu.VMEM((1, LANES), F32)],
        compiler_params=_cparams(("arbitrary", "arbitrary")),
        name="past_cache",
    )(k_past.reshape(b, p, WIDTH).astype(F32), v_past.reshape(b, p, WIDTH).astype(F32), lf, tri, place_k, ones_k)


def _attn_kernel(q_ref, k_ref, vt_ref, o_ref, m_ref, alpha_ref, s_ref, p_ref, acc_ref, *, tq, tk, past):
    qi = pl.program_id(2)
    q_first = past + qi * tq
    n_full = q_first // tk

    m_ref[...] = jnp.full_like(m_ref, -jnp.inf)
    alpha_ref[...] = jnp.ones_like(alpha_ref)
    p_ref[1] = jnp.zeros(p_ref.shape[1:], p_ref.dtype)
    acc_ref[...] = jnp.zeros_like(acc_ref)

    def scores(j, slot):
        k_start = pl.multiple_of(j * tk, tk)
        for hb in range(2):
            s_ref[slot, hb] = lax.dot_general(k_ref[0, hb, pl.ds(k_start, tk), :], q_ref[0, hb],
                                              (((1,), (1,)), ((), ())), preferred_element_type=F32)

    def add_values(j, slot):
        k_start = pl.multiple_of(jnp.maximum(j, 0) * tk, tk)
        for hb in range(2):
            acc_ref[hb] = (alpha_ref[slot, hb] * acc_ref[hb]
                           + _dot(vt_ref[0, hb, :, pl.ds(k_start, tk)], p_ref[slot, hb]))

    def softmax(j, slot, masked):
        for hb in range(2):
            s = s_ref[slot, hb]
            if masked:
                k_pos = j * tk + lax.broadcasted_iota(jnp.int32, (tk, tq), 0)
                q_pos = q_first + lax.broadcasted_iota(jnp.int32, (tk, tq), 1)
                s = jnp.where(k_pos <= q_pos, s, -jnp.inf)
            m_prev = m_ref[hb]
            m_new = jnp.maximum(m_prev, jnp.max(s, axis=0, keepdims=True))
            alpha_ref[slot, hb] = jnp.exp2(m_prev - m_new)
            p_ref[slot, hb] = jnp.exp2(s - m_new).astype(BF16)
            m_ref[hb] = m_new

    def step(j, slot, last):
        add_values(j - 1, 1 - slot)
        softmax(j, slot, masked=last)
        if not last:
            scores(j + 1, 1 - slot)

    def block_pair(jj, carry):
        step(2 * jj, 0, False)
        step(2 * jj + 1, 1, False)
        return carry

    scores(0, 0)
    lax.fori_loop(0, n_full // 2, block_pair, 0)

    @pl.when(n_full % 2 == 1)
    def _():
        step(n_full - 1, 0, False)
        step(n_full, 1, True)
        add_values(n_full, 1)

    @pl.when(n_full % 2 == 0)
    def _():
        step(n_full, 0, True)
        add_values(n_full, 0)

    heads = [acc_ref[hb, 0:HEAD_DIM, :] / acc_ref[hb, HEAD_DIM:HEAD_DIM + 1, :] for hb in range(2)]
    o_ref[0] = jnp.concatenate(heads, axis=0).T.astype(o_ref.dtype)


def _fox_attention(q_aug, k_aug, v_t, past, tq, tk):
    b, _, t_q, _ = q_aug.shape
    t_k = k_aug.shape[2]
    assert t_q % tq == 0 and t_k % tk == 0 and tk % tq == 0 and past % tq == 0
    return pl.pallas_call(
        functools.partial(_attn_kernel, tq=tq, tk=tk, past=past),
        out_shape=jax.ShapeDtypeStruct((b, t_q, WIDTH), BF16),
        grid=(b, N_PAIRS, t_q // tq),
        in_specs=[pl.BlockSpec((1, 2, tq, LANES), lambda i, p, qi: (i, p, qi, 0)),
                  pl.BlockSpec((1, 2, t_k, LANES), lambda i, p, qi: (i, p, 0, 0)),
                  pl.BlockSpec((1, 2, V_ROWS, t_k), lambda i, p, qi: (i, p, 0, 0))],
        out_specs=pl.BlockSpec((1, tq, LANES), lambda i, p, qi: (i, qi, p)),
        scratch_shapes=[pltpu.VMEM((2, 1, tq), F32), pltpu.VMEM((2, 2, 1, tq), F32),
                        pltpu.VMEM((2, 2, tk, tq), F32), pltpu.VMEM((2, 2, tk, tq), BF16),
                        pltpu.VMEM((2, V_ROWS, tq), F32)],
        compiler_params=_cparams(("arbitrary", "arbitrary", "arbitrary")),
        name="fox_attention",
    )(q_aug, k_aug, v_t)


def _out_kernel(x_ref, ys_ref, g_ref, bonus_ref, yf_ref, mod_ref, gng_ref, gnb_ref, bd_ref, wo_ref, n2_ref,
                wg_ref, wu_ref, wd_ref, o_ref):
    bd = bd_ref[...]
    gt1 = mod_ref[0, 2:3, :]
    sh2 = mod_ref[0, 3:4, :]
    sc2 = mod_ref[0, 4:5, :]
    gt2 = mod_ref[0, 5:6, :]
    inv_hd = 1.0 / HEAD_DIM
    ys = ys_ref[0]
    d = ys - _head_sums(ys, bd) * inv_hd
    var = _head_sums(d * d, bd) * inv_hd
    yn = d * lax.rsqrt(var + GN_EPS) * gng_ref[...] + gnb_ref[...]
    y_rwkv = ((yn + bonus_ref[0]) * g_ref[0]).astype(BF16)
    mix = _dot(y_rwkv, wo_ref[0:WIDTH, :]) + _dot(yf_ref[0], wo_ref[WIDTH:2 * WIDTH, :])
    x1 = x_ref[0] + gt1 * mix
    h2 = x1 * lax.rsqrt(jnp.mean(x1 * x1, axis=-1, keepdims=True) + NORM_EPS) * n2_ref[...]
    hb = (h2 * (1.0 + sc2) + sh2).astype(BF16)
    act = (_silu(_dot(hb, wg_ref[...])) * _dot(hb, wu_ref[...])).astype(BF16)
    o_ref[0] = x1 + gt2 * _dot(act, wd_ref[...])


def _outproj_ffn(x, ys, g, bonus, yf, modp, wts, tm):
    b, t, _ = x.shape
    tile = lambda w: pl.BlockSpec((1, tm, w), lambda i, j: (i, j, 0))
    consts = [wts[n] for n in ("gn_g", "gn_b", "bd", "w_out", "norm2_g", "w_ffn_gate", "w_ffn_up", "w_ffn_down")]
    return pl.pallas_call(
        _out_kernel,
        out_shape=jax.ShapeDtypeStruct((b, t, D_MODEL), F32),
        grid=(b, t // tm),
        in_specs=([tile(D_MODEL), tile(WIDTH), tile(WIDTH), tile(WIDTH), tile(WIDTH),
                   pl.BlockSpec((1, 6, D_MODEL), lambda i, j: (i, 0, 0))]
                  + [_const_spec(c.shape) for c in consts]),
        out_specs=tile(D_MODEL),
        compiler_params=_cparams(("arbitrary", "arbitrary")),
        name="outproj_ffn",
    )(x, ys, g, bonus, yf, modp, *consts)


def _prep_weights(norm1_g, w_in, shift_mu, w0, w_decay_up, a0, w_aaa_up, w_gate_up, k_k, k_a, r_k, gn_g, gn_b,
                  fox_q_g, fox_k_g, fox_f_b, w_out, norm2_g, w_ffn_gate, w_ffn_up, w_ffn_down):
    row = lambda a: a.reshape(1, -1).astype(F32)
    fox0 = RWKV_COLS
    zeros_lora = jnp.zeros((DECAY_LORA, WIDTH), BF16)
    head = jnp.arange(WIDTH) // HEAD_DIM
    return {
        "norm1_g": row(norm1_g),
        "w_r": w_in[:, :RWKV_COLS].astype(BF16),
        "w_f": w_in[:, fox0:fox0 + 3 * WIDTH].astype(BF16),
        "w_fl": jnp.pad(w_in[:, fox0 + 3 * WIDTH:], ((0, 0), (0, LANES - N_HEADS))).astype(BF16),
        "shift_mu": row(shift_mu), "w0": row(w0), "a0": row(a0),
        "w_du": jnp.concatenate([w_decay_up.astype(BF16), zeros_lora], axis=0),
        "w_au": jnp.concatenate([zeros_lora, w_aaa_up.astype(BF16)], axis=0),
        "w_gu": w_gate_up.astype(BF16),
        "k_k": row(k_k), "k_a": row(k_a), "r_k": row(r_k),
        "fox_q_g": row(jnp.tile(fox_q_g, N_HEADS)), "fox_k_g": row(jnp.tile(fox_k_g, N_HEADS)),
        "fox_f_b": jnp.pad(row(fox_f_b), ((0, 0), (0, LANES - N_HEADS))),
        "bd": (head[:256, None] == head[None, :256]).astype(BF16),
        "gn_g": row(gn_g), "gn_b": row(gn_b),
        "w_out": w_out.astype(BF16), "norm2_g": row(norm2_g),
        "w_ffn_gate": w_ffn_gate.astype(BF16), "w_ffn_up": w_ffn_up.astype(BF16),
        "w_ffn_down": w_ffn_down.astype(BF16),
    }


def _to_pairs(state):
    b = state.shape[0]
    s = state.reshape(b, N_PAIRS, 2, HEAD_DIM, HEAD_DIM)
    return jnp.transpose(s, (0, 1, 4, 2, 3)).reshape(b, N_PAIRS, HEAD_DIM, LANES)


def _from_pairs(state):
    b = state.shape[0]
    s = state.reshape(b, N_PAIRS, HEAD_DIM, 2, HEAD_DIM)
    return jnp.transpose(s, (0, 1, 3, 4, 2)).reshape(b, N_HEADS, HEAD_DIM, HEAD_DIM)


def _round_up(n, m):
    return -(-n // m) * m


def _layer(x, modp, shift_prev, s_prev, k_past, v_past, logf_past, wts):
    b, t, _ = x.shape
    past = k_past.shape[1]
    tm, tm_out = min(PROJ_ROWS, t), min(OUT_ROWS, t)
    if past:
        k_aug_past, v_t_past, lc_in = _past_cache(k_past, v_past, logf_past)
    else:
        lc_in = jnp.zeros((b, 1, LANES), F32)
    (r, w, k2, v, kk, bvec, g, bonus, q_aug, k_aug, v_t, kf, vf, lf_pad, new_shift) = _inproj(
        x, modp, shift_prev, lc_in, wts, tm)

    ys, s_new = _rwkv_scan(r, w, k2, v, kk, bvec, _to_pairs(s_prev.astype(F32)))

    logf = lf_pad[:, :, :N_HEADS]
    t_q = _round_up(t, LANES)
    q_aug = jnp.pad(q_aug, ((0, 0), (0, 0), (0, t_q - t), (0, 0)))
    tq = min(ATTN_QUERY_BLOCK, t_q)
    if past:
        tk_all = _round_up(past + t_q, LANES)
        tail = tk_all - past - t
        k_aug = jnp.concatenate([k_aug_past, k_aug, jnp.zeros((b, N_HEADS, tail, LANES), BF16)], axis=2)
        v_t = jnp.concatenate([v_t_past, v_t, jnp.zeros((b, N_HEADS, V_ROWS, tail), BF16)], axis=3)
        tk = tk_all if t_q == tq else tq
    else:
        tk = max(tq, ATTN_KEY_BLOCK)
    yf = _fox_attention(q_aug, k_aug, v_t, past, tq, tk)[:, :t]

    y = _outproj_ffn(x, ys, g, bonus, yf, modp, wts, tm_out)
    return (y, _from_pairs(s_new), new_shift, kf.reshape(b, t, N_HEADS, HEAD_DIM),
            vf.reshape(b, t, N_HEADS, HEAD_DIM), logf)


def kernel(x_prompt, x_sample, cache_fox_k, cache_fox_v, cache_fox_logf, state_rwkv, state_rwkv_shift, c_prompt, c_sample, norm1_g, w_ada, b_ada, w_in, shift_mu, w0, w_decay_up, a0, w_aaa_up, w_gate_up, k_k, k_a, r_k, gn_g, gn_b, fox_q_g, fox_k_g, fox_f_b, w_out, norm2_g, w_ffn_gate, w_ffn_up, w_ffn_down):
    depth = w_in.shape[0]
    bp, bs = x_prompt.shape[0], x_sample.shape[0]
    zero_shift = jnp.zeros((bp, 1, RWKV_COLS), F32)
    zero_state = jnp.zeros((bp, N_HEADS, HEAD_DIM, HEAD_DIM), F32)
    zero_kv = jnp.zeros((bp, 0, N_HEADS, HEAD_DIM), F32)
    zero_logf = jnp.zeros((bp, 0, N_HEADS), F32)
    hp, hs = x_prompt, x_sample
    outs_p, outs_s = [], []
    for l in range(depth):
        wts = _prep_weights(norm1_g[l], w_in[l], shift_mu[l], w0[l], w_decay_up[l], a0[l], w_aaa_up[l],
                            w_gate_up[l], k_k[l], k_a[l], r_k[l], gn_g[l], gn_b[l], fox_q_g[l], fox_k_g[l],
                            fox_f_b[l], w_out[l], norm2_g[l], w_ffn_gate[l], w_ffn_up[l], w_ffn_down[l])
        rows = _round_up(bp + bs, SUBLANES)
        c_all = jnp.pad(jnp.concatenate([c_prompt, c_sample], axis=0), ((0, rows - bp - bs), (0, 0)))
        mod = _adaln(c_all, w_ada[l].astype(BF16), b_ada[l].reshape(1, -1)).reshape(rows, 6, D_MODEL)
        res_p = _layer(hp, mod[:bp], zero_shift, zero_state, zero_kv, zero_kv, zero_logf, wts)
        res_s = _layer(hs, mod[bp:bp + bs], state_rwkv_shift[l], state_rwkv[l], cache_fox_k[l], cache_fox_v[l],
                       cache_fox_logf[l], wts)
        hp, hs = res_p[0], res_s[0]
        outs_p.append(res_p[1:])
        outs_s.append(res_s[1:])
    stack = lambda outs, i: jnp.stack([o[i] for o in outs])
    return (hp, hs,
            stack(outs_p, 0), stack(outs_p, 1), stack(outs_p, 2), stack(outs_p, 3), stack(outs_p, 4),
            stack(outs_s, 0), stack(outs_s, 1), stack(outs_s, 2), stack(outs_s, 3), stack(outs_s, 4))
```
